```python
import math
import jax
import jax.numpy as jnp
from jax import lax
import numpy as np

D_MODEL = 4096
BATCH = 4
SEQ = 2048
DEPTH = 2

CHUNK = 64
N_META = 16
Q_BLOCK = 128

DIFF_HEADS = 8
DIFF_QK_DIM = 128
DIFF_V_DIM = 2 * DIFF_QK_DIM
ATTN_WIDTH = DIFF_HEADS * DIFF_V_DIM
SUBLN_EPS = 1e-5

SSM_WIDTH = D_MODEL // 2
SSM_HEAD_DIM = 64
SSM_HEADS = SSM_WIDTH // SSM_HEAD_DIM
SSM_GROUPS = 4
SSM_STATE = 128
SSM_CONV = 4
SSD_CHUNK = 64
XBC_WIDTH = SSM_WIDTH + 2 * SSM_GROUPS * SSM_STATE
DT_MIN = 0.001
DT_MAX = 0.1

Q_COLS = DIFF_HEADS * 2 * DIFF_QK_DIM
K_COLS = Q_COLS
V_COLS = ATTN_WIDTH
IN_COLS = Q_COLS + K_COLS + V_COLS + SSM_WIDTH + XBC_WIDTH + SSM_HEADS + 2 * D_MODEL
SPLIT_POINTS = (
    Q_COLS,
    Q_COLS + K_COLS,
    Q_COLS + K_COLS + V_COLS,
    Q_COLS + K_COLS + V_COLS + SSM_WIDTH,
    Q_COLS + K_COLS + V_COLS + SSM_WIDTH + XBC_WIDTH,
    Q_COLS + K_COLS + V_COLS + SSM_WIDTH + XBC_WIDTH + SSM_HEADS,
    Q_COLS + K_COLS + V_COLS + SSM_WIDTH + XBC_WIDTH + SSM_HEADS + D_MODEL,
)

D_FF = 11008
N_EXPERTS = 8
TOP_K = 2
D_FF_EXPERT = 4096
N_DENSE = (DEPTH + 1) // 2
N_MOE = DEPTH // 2

NORM_EPS = 1e-6
NEG_BIG = -1e30

kernel_name = 'chunk_causal_hybrid_diffattn_ssd_moe'


def rmsnorm(x, w, eps=NORM_EPS):
    xf = x.astype(jnp.float32)
    y = xf * lax.rsqrt(jnp.mean(xf * xf, axis=-1, keepdims=True) + eps)
    return (y * w.astype(jnp.float32)).astype(x.dtype)


def chunk_ids(n):
    p = jnp.arange(n)
    return jnp.where(p < N_META, 0, 1 + (p - N_META) // CHUNK)


def chunk_end(p):
    if p < N_META:
        return N_META
    return N_META + CHUNK * ((p - N_META) // CHUNK + 1)


def alibi_slopes(n_heads):
    return jnp.exp2(-8.0 * (jnp.arange(n_heads, dtype=jnp.float32) + 1.0) / n_heads)


def diff_attention(q, k, v, lam, subln_w, lambda_init):
    bsz, n = q.shape[0], q.shape[1]
    cid = chunk_ids(n)
    pos = jnp.arange(n, dtype=jnp.float32)
    slopes = alibi_slopes(DIFF_HEADS)
    scale = DIFF_QK_DIM ** -0.5
    qf, kf, vf = q.astype(jnp.float32), k.astype(jnp.float32), v.astype(jnp.float32)
    outs = []
    for blk in range(n // Q_BLOCK):
        q0, q1 = blk * Q_BLOCK, (blk + 1) * Q_BLOCK
        kv_end = min(n, chunk_end(q1 - 1))
        s = jnp.einsum('bqhmd,bkhmd->bhmqk', qf[:, q0:q1], kf[:, :kv_end]) * scale
        dist = jnp.abs(pos[q0:q1, None] - pos[None, :kv_end])
        s = s - (slopes[:, None, None] * dist)[None, :, None]
        mask = cid[q0:q1, None] >= cid[None, :kv_end]
        s = jnp.where(mask, s, NEG_BIG)
        p = jax.nn.softmax(s, axis=-1)
        a = p[:, :, 0] - lam * p[:, :, 1]
        outs.append(jnp.einsum('bhqk,bkhd->bqhd', a, vf[:, :kv_end]))
    o = jnp.concatenate(outs, axis=1)
    o = rmsnorm(o, subln_w, SUBLN_EPS) * (1.0 - lambda_init)
    return o.reshape(bsz, n, ATTN_WIDTH)


def causal_depthwise_conv(u, w, b):
    ksz, ch = w.shape
    y = lax.conv_general_dilated(
        u, w[:, None, :].astype(u.dtype), window_strides=(1,), padding=[(ksz - 1, 0)],
        dimension_numbers=('NWC', 'WIO', 'NWC'), feature_group_count=ch)
    return y + b.astype(y.dtype)


def segsum(x):
    t = x.shape[-1]
    xr = jnp.broadcast_to(x[..., None], x.shape + (t,))
    xr = jnp.where(jnp.tril(jnp.ones((t, t), dtype=bool), -1), xr, 0.0)
    cs = jnp.cumsum(xr, axis=-2)
    return jnp.where(jnp.tril(jnp.ones((t, t), dtype=bool)), cs, -jnp.inf)


def ssd_chunked_scan(x, dt, a, b, c):
    bsz, n = x.shape[0], x.shape[1]
    nc = n // SSD_CHUNK
    r = SSM_HEADS // SSM_GROUPS
    xd = (x * dt[..., None]).reshape(bsz, nc, SSD_CHUNK, SSM_GROUPS, r, SSM_HEAD_DIM)
    adt = (dt * a).reshape(bsz, nc, SSD_CHUNK, SSM_GROUPS, r).transpose(0, 3, 4, 1, 2)
    a_cum = jnp.cumsum(adt, axis=-1)
    bc = b.reshape(bsz, nc, SSD_CHUNK, SSM_GROUPS, SSM_STATE)
    cc = c.reshape(bsz, nc, SSD_CHUNK, SSM_GROUPS, SSM_STATE)
    lmat = jnp.exp(segsum(adt))
    cb = jnp.einsum('bclgn,bcsgn->bgcls', cc, bc)
    y_diag = jnp.einsum('bgcls,bgrcls,bcsgrp->bclgrp', cb, lmat, xd)
    decay_states = jnp.exp(a_cum[..., -1:] - a_cum)
    states = jnp.einsum('bclgn,bgrcl,bclgrp->bcgrpn', bc, decay_states, xd)
    states = jnp.concatenate([jnp.zeros_like(states[:, :1]), states], axis=1)
    a_last = jnp.pad(a_cum[..., -1], ((0, 0), (0, 0), (0, 0), (1, 0)))
    chunk_decay = jnp.exp(segsum(a_last))
    new_states = jnp.einsum('bgrzc,bcgrpn->bzgrpn', chunk_decay, states)
    prev_states = new_states[:, :-1]
    y_off = jnp.einsum('bclgn,bcgrpn,bgrcl->bclgrp', cc, prev_states, jnp.exp(a_cum))
    return (y_diag + y_off).reshape(bsz, n, SSM_HEADS, SSM_HEAD_DIM)


def ssd_mixer(z, xbc, dt_raw, conv_w, conv_b, dt_bias, a_log, d_skip, ssm_norm):
    bsz, n = xbc.shape[0], xbc.shape[1]
    xbc = jax.nn.silu(causal_depthwise_conv(xbc, conv_w, conv_b))
    xs, bs, cs = jnp.split(xbc, [SSM_WIDTH, SSM_WIDTH + SSM_GROUPS * SSM_STATE], axis=-1)
    xs = xs.reshape(bsz, n, SSM_HEADS, SSM_HEAD_DIM).astype(jnp.float32)
    bs = bs.reshape(bsz, n, SSM_GROUPS, SSM_STATE).astype(jnp.float32)
    cs = cs.reshape(bsz, n, SSM_GROUPS, SSM_STATE).astype(jnp.float32)
    dt = jax.nn.softplus(dt_raw.astype(jnp.float32) + dt_bias.astype(jnp.float32))
    a = -jnp.exp(a_log.astype(jnp.float32))
    y = ssd_chunked_scan(xs, dt, a, bs, cs) + d_skip.astype(jnp.float32)[:, None] * xs
    g = y.reshape(bsz, n, SSM_WIDTH) * jax.nn.silu(z.astype(jnp.float32))
    gs = g.reshape(bsz, n, SSM_GROUPS, SSM_WIDTH // SSM_GROUPS)
    gs = gs * lax.rsqrt(jnp.mean(gs * gs, axis=-1, keepdims=True) + NORM_EPS)
    return gs.reshape(bsz, n, SSM_WIDTH) * ssm_norm.astype(jnp.float32)


def hybrid_mixer(h, w_in, conv_w, conv_b, dt_bias, a_log, d_skip, ssm_norm,
                 lam_q1, lam_k1, lam_q2, lam_k2, diff_norm, w_proj_attn, w_proj_ssm, w_out,
                 lambda_init):
    bsz, n, _ = h.shape
    n_pad = -(-n // Q_BLOCK) * Q_BLOCK
    hp = jnp.pad(h, ((0, 0), (0, n_pad - n), (0, 0)))
    proj = hp @ w_in
    q, k, v, z, xbc, dt_raw, g_attn, g_ssm = jnp.split(proj, SPLIT_POINTS, axis=-1)
    q = q.reshape(bsz, n_pad, DIFF_HEADS, 2, DIFF_QK_DIM)
    k = k.reshape(bsz, n_pad, DIFF_HEADS, 2, DIFF_QK_DIM)
    v = v.reshape(bsz, n_pad, DIFF_HEADS, DIFF_V_DIM)
    lam = (jnp.exp(jnp.sum(lam_q1.astype(jnp.float32) * lam_k1.astype(jnp.float32)))
           - jnp.exp(jnp.sum(lam_q2.astype(jnp.float32) * lam_k2.astype(jnp.float32)))
           + lambda_init)
    o_attn = diff_attention(q, k, v, lam, diff_norm, lambda_init)[:, :n]
    o_ssm = ssd_mixer(z, xbc, dt_raw, conv_w, conv_b, dt_bias, a_log, d_skip, ssm_norm)[:, :n]
    gate_a = jax.nn.sigmoid(g_attn[:, :n].astype(jnp.float32))
    gate_s = jax.nn.sigmoid(g_ssm[:, :n].astype(jnp.float32))
    merged = gate_a * (o_attn @ w_proj_attn) + gate_s * (o_ssm @ w_proj_ssm)
    return merged @ w_out


def swiglu(h, w_gate, w_up, w_down):
    return (jax.nn.silu(h @ w_gate) * (h @ w_up)) @ w_down


def moe_swiglu(h, router, w_gate, w_up, w_down):
    logits = (h @ router).astype(jnp.float32)
    top_vals, top_idx = lax.top_k(logits, TOP_K)
    weights = jax.nn.softmax(top_vals, axis=-1)
    combine = jnp.sum(jax.nn.one_hot(top_idx, N_EXPERTS, dtype=jnp.float32) * weights[..., None], axis=-2)
    out = jnp.zeros(h.shape, jnp.float32)
    for e in range(N_EXPERTS):
        out = out + combine[..., e:e + 1] * swiglu(h, w_gate[e], w_up[e], w_down[e])
    return out


def setup_inputs(seed: int = 0) -> dict:
    key = jax.random.key(seed)
    ks = jax.random.split(key, 28)
    f32 = jnp.float32

    def nrm(k, shape, scale):
        return jax.random.normal(k, shape, f32) * scale

    def gain(k, shape):
        return 1.0 + 0.01 * jax.random.normal(k, shape, f32)

    L = DEPTH
    dt0 = jnp.exp(jax.random.uniform(ks[7], (L, SSM_HEADS), f32, math.log(DT_MIN), math.log(DT_MAX)))
    dt_bias = dt0 + jnp.log(-jnp.expm1(-dt0))
    a_log = jnp.log(jax.random.uniform(ks[8], (L, SSM_HEADS), f32, 1.0, 16.0))
    return {
        'x': nrm(ks[0], (BATCH, SEQ, D_MODEL), 1.0),
        'meta_tokens': nrm(ks[1], (N_META, D_MODEL), 1.0),
        'mix_norm': gain(ks[2], (L, D_MODEL)),
        'mix_w_in': nrm(ks[3], (L, D_MODEL, IN_COLS), D_MODEL ** -0.5),
        'conv_w': nrm(ks[4], (L, SSM_CONV, XBC_WIDTH), SSM_CONV ** -0.5),
        'conv_b': nrm(ks[5], (L, XBC_WIDTH), 0.01),
        'dt_bias': dt_bias,
        'a_log': a_log,
        'd_skip': gain(ks[6], (L, SSM_HEADS)),
        'ssm_norm': gain(ks[9], (L, SSM_WIDTH)),
        'lambda_q1': nrm(ks[10], (L, DIFF_QK_DIM), 0.1),
        'lambda_k1': nrm(ks[11], (L, DIFF_QK_DIM), 0.1),
        'lambda_q2': nrm(ks[12], (L, DIFF_QK_DIM), 0.1),
        'lambda_k2': nrm(ks[13], (L, DIFF_QK_DIM), 0.1),
        'diff_norm': gain(ks[14], (L, DIFF_V_DIM)),
        'w_proj_attn': nrm(ks[15], (L, ATTN_WIDTH, D_MODEL), ATTN_WIDTH ** -0.5),
        'w_proj_ssm': nrm(ks[16], (L, SSM_WIDTH, D_MODEL), SSM_WIDTH ** -0.5),
        'w_out': nrm(ks[17], (L, D_MODEL, D_MODEL), D_MODEL ** -0.5),
        'ffn_norm': gain(ks[18], (L, D_MODEL)),
        'ffn_w_gate': nrm(ks[19], (N_DENSE, D_MODEL, D_FF), D_MODEL ** -0.5),
        'ffn_w_up': nrm(ks[20], (N_DENSE, D_MODEL, D_FF), D_MODEL ** -0.5),
        'ffn_w_down': nrm(ks[21], (N_DENSE, D_FF, D_MODEL), D_FF ** -0.5),
        'router': nrm(ks[22], (N_MOE, D_MODEL, N_EXPERTS), D_MODEL ** -0.5),
        'moe_w_gate': nrm(ks[23], (N_MOE, N_EXPERTS, D_MODEL, D_FF_EXPERT), D_MODEL ** -0.5),
        'moe_w_up': nrm(ks[24], (N_MOE, N_EXPERTS, D_MODEL, D_FF_EXPERT), D_MODEL ** -0.5),
        'moe_w_down': nrm(ks[25], (N_MOE, N_EXPERTS, D_FF_EXPERT, D_MODEL), D_FF_EXPERT ** -0.5),
        'final_norm': gain(ks[26], (D_MODEL,)),
    }


def reference(x, meta_tokens, mix_norm, mix_w_in, conv_w, conv_b, dt_bias, a_log, d_skip, ssm_norm,
              lambda_q1, lambda_k1, lambda_q2, lambda_k2, diff_norm, w_proj_attn, w_proj_ssm, w_out,
              ffn_norm, ffn_w_gate, ffn_w_up, ffn_w_down, router, moe_w_gate, moe_w_up, moe_w_down,
              final_norm):
    bsz = x.shape[0]
    meta = jnp.broadcast_to(meta_tokens[None].astype(x.dtype), (bsz, N_META, D_MODEL))
    hcur = jnp.concatenate([meta, x], axis=1)
    for layer in range(DEPTH):
        lambda_init = 0.8 - 0.6 * math.exp(-0.3 * layer)
        hn = rmsnorm(hcur, mix_norm[layer])
        hcur = hcur + hybrid_mixer(
            hn, mix_w_in[layer], conv_w[layer], conv_b[layer], dt_bias[layer], a_log[layer],
            d_skip[layer], ssm_norm[layer], lambda_q1[layer], lambda_k1[layer], lambda_q2[layer],
            lambda_k2[layer], diff_norm[layer], w_proj_attn[layer], w_proj_ssm[layer], w_out[layer],
            lambda_init)
        hn = rmsnorm(hcur, ffn_norm[layer])
        if layer % 2 == 0:
            i = layer // 2
            hcur = hcur + swiglu(hn, ffn_w_gate[i], ffn_w_up[i], ffn_w_down[i])
        else:
            i = layer // 2
            hcur = hcur + moe_swiglu(hn, router[i], moe_w_gate[i], moe_w_up[i], moe_w_down[i])
    return rmsnorm(hcur, final_norm)[:, N_META:]
```

```python
import functools
import math

import jax
import jax.numpy as jnp
from jax import lax
from jax.experimental import pallas as pl
from jax.experimental.pallas import tpu as pltpu

F32 = jnp.float32
BF16 = jnp.bfloat16

LANES = 128
VMEM_LIMIT_BYTES_V7X = 56 * 2**20

D_MODEL = 4096
N_META = 16
CHUNK = 64
Q_BLOCK = 128
HEADS = 8
QK_DIM = 128
V_DIM = 256
ATTN_WIDTH = HEADS * V_DIM
SUBLN_EPS = 1e-5
SSM_WIDTH = 2048
SSM_HEAD_DIM = 64
SSM_HEADS = 32
SSM_GROUPS = 4
SSM_STATE = 128
SSM_CONV = 4
SSM_GROUP_WIDTH = SSM_WIDTH // SSM_GROUPS
XBC_WIDTH = SSM_WIDTH + 2 * SSM_GROUPS * SSM_STATE
QKV_COLS = 3 * ATTN_WIDTH
ZX_COL0 = QKV_COLS
ZX_COLS = SSM_WIDTH + XBC_WIDTH
DT_COL0 = ZX_COL0 + ZX_COLS
GATE_COL0 = DT_COL0 + SSM_HEADS
N_EXPERTS = 8
NORM_EPS = 1e-6
NEG_BIG = -1e30
SSD_BLOCK = 128

ROW_TILE = 1088
MOE_ROW_TILE = 256
NORM_ROW_TILE = 272


def _silu(x):
    return x * (1.0 / (1.0 + jnp.exp(-x)))


def _sigmoid(x):
    return 1.0 / (1.0 + jnp.exp(-x))


def _compiler_params(semantics):
    return pltpu.CompilerParams(dimension_semantics=semantics, vmem_limit_bytes=VMEM_LIMIT_BYTES_V7X)


def _rmsnorm_kernel(x_ref, w_ref, o_ref, *, eps):
    x = x_ref[...]
    ms = jnp.mean(x * x, axis=-1, keepdims=True)
    o_ref[...] = (x * lax.rsqrt(ms + eps) * w_ref[...]).astype(o_ref.dtype)


def _norm_row_tile(m):
    return NORM_ROW_TILE if m % NORM_ROW_TILE == 0 else Q_BLOCK


def rmsnorm(x, w, *, eps=NORM_EPS, out_dtype=BF16):
    m, d = x.shape
    tr = _norm_row_tile(m)
    return pl.pallas_call(
        functools.partial(_rmsnorm_kernel, eps=eps),
        grid=(m // tr,),
        in_specs=[pl.BlockSpec((tr, d), lambda i: (i, 0)), pl.BlockSpec((1, d), lambda i: (0, 0))],
        out_specs=pl.BlockSpec((tr, d), lambda i: (i, 0)),
        out_shape=jax.ShapeDtypeStruct((m, d), out_dtype),
        compiler_params=_compiler_params(("parallel",)),
    )(x, w.reshape(1, d))


def _split_bf16(x):
    hi = x.astype(BF16)
    lo = (x - hi.astype(F32)).astype(BF16)
    return hi, lo


def _router_kernel(x_ref, w_ref, r_ref, o_ref, *, eps):
    x = x_ref[...]
    ms = jnp.mean(x * x, axis=-1, keepdims=True)
    hn = x * lax.rsqrt(ms + eps) * w_ref[...]
    h_hi, h_lo = _split_bf16(hn)
    r_hi, r_lo = _split_bf16(r_ref[...])
    logits = (jnp.dot(h_hi, r_hi, preferred_element_type=F32)
              + jnp.dot(h_hi, r_lo, preferred_element_type=F32)
              + jnp.dot(h_lo, r_hi, preferred_element_type=F32))
    lane = lax.broadcasted_iota(jnp.int32, logits.shape, 1).astype(F32)
    lg = jnp.where(lane < N_EXPERTS, logits, -jnp.inf)
    m1 = jnp.max(lg, axis=-1, keepdims=True)
    i1 = jnp.min(jnp.where(lg == m1, lane, float(LANES)), axis=-1, keepdims=True)
    lg2 = jnp.where(lane == i1, -jnp.inf, lg)
    m2 = jnp.max(lg2, axis=-1, keepdims=True)
    i2 = jnp.min(jnp.where(lg2 == m2, lane, float(LANES)), axis=-1, keepdims=True)
    e2 = jnp.exp(m2 - m1)
    den = 1.0 + e2
    out = jnp.where(lane == i1, 1.0 / den, 0.0) + jnp.where(lane == i2, e2 / den, 0.0)
    out = jnp.where(lane == N_EXPERTS, i1, out)
    out = jnp.where(lane == N_EXPERTS + 1, i2, out)
    o_ref[...] = out


def route_tokens(x, w, r_pad):
    m, d = x.shape
    tr = _norm_row_tile(m)
    return pl.pallas_call(
        functools.partial(_router_kernel, eps=NORM_EPS),
        grid=(m // tr,),
        in_specs=[pl.BlockSpec((tr, d), lambda i: (i, 0)), pl.BlockSpec((1, d), lambda i: (0, 0)),
                  pl.BlockSpec((d, LANES), lambda i: (0, 0))],
        out_specs=pl.BlockSpec((tr, LANES), lambda i: (i, 0)),
        out_shape=jax.ShapeDtypeStruct((m, LANES), F32),
        compiler_params=_compiler_params(("parallel",)),
    )(x, w.reshape(1, d), r_pad)


def _mm_kernel(grp_ref, nused_ref, *refs, n_a, w_to_a, cast, n_extra, nk, epilogue):
    n_w = len(w_to_a)
    a_refs = refs[:n_a]
    w_refs = refs[n_a:n_a + n_w]
    e_refs = refs[n_a + n_w:n_a + n_w + n_extra]
    o_ref = refs[n_a + n_w + n_extra]
    scratch = list(refs[n_a + n_w + n_extra + 1:])
    wb_refs = [scratch.pop(0) if c else None for c in cast]
    acc_refs = [scratch.pop(0) for _ in range(n_w)] if nk > 1 else []
    i = pl.program_id(1)
    k = pl.program_id(2)

    @pl.when(i >= nused_ref[0])
    def _():
        o_ref[...] = jnp.zeros_like(o_ref)

    @pl.when(i < nused_ref[0])
    def _():
        if any(cast):
            def do_cast():
                for w_ref, wb_ref in zip(w_refs, wb_refs):
                    if wb_ref is not None:
                        wb_ref[...] = w_ref[...].astype(BF16)
            if nk == 1:
                changed = jnp.logical_or(i == 0, grp_ref[i] != grp_ref[jnp.maximum(i - 1, 0)])
                pl.when(changed)(do_cast)
            else:
                do_cast()
        parts = []
        for wi in range(n_w):
            w = wb_refs[wi][...] if cast[wi] else w_refs[wi][...]
            parts.append(jnp.dot(a_refs[w_to_a[wi]][...], w, preferred_element_type=F32))
        if nk == 1:
            o_ref[...] = epilogue(parts, e_refs).astype(o_ref.dtype)
        else:
            @pl.when(k == 0)
            def _():
                for acc, p in zip(acc_refs, parts):
                    acc[...] = p

            @pl.when(k > 0)
            def _():
                for acc, p in zip(acc_refs, parts):
                    acc[...] += p

            @pl.when(k == nk - 1)
            def _():
                o_ref[...] = epilogue([acc[...] for acc in acc_refs], e_refs).astype(o_ref.dtype)


def grouped_matmul(a_list, w_list, w_to_a, extras, epilogue, *, grp, nused, ncols, tm, tn, out_dtype,
                   w_col0=0, extra_col0=(), tk=None):
    m, kdim = a_list[0].shape
    tk = kdim if tk is None else tk
    nk = kdim // tk
    assert m % tm == 0 and ncols % tn == 0 and kdim % tk == 0 and w_col0 % tn == 0
    assert all(a.shape == (m, kdim) for a in a_list) and all(w.shape[1] == kdim for w in w_list)
    cast = tuple(w.dtype != BF16 for w in w_list)
    wcb = w_col0 // tn
    ecb = tuple(c // tn for c in extra_col0)

    def row(i, n):
        return jnp.minimum(i, n[0] - 1)

    in_specs = [pl.BlockSpec((tm, tk), lambda j, i, k, g, n: (row(i, n), k)) for _ in a_list]
    in_specs += [pl.BlockSpec((None, tk, tn), lambda j, i, k, g, n: (g[row(i, n)], k, wcb + j)) for _ in w_list]
    in_specs += [pl.BlockSpec((tm, tn), functools.partial(lambda j, i, k, g, n, eb: (row(i, n), eb + j), eb=eb))
                 for eb in ecb]
    scratch = [pltpu.VMEM((tk, tn), BF16) for c in cast if c]
    if nk > 1:
        scratch += [pltpu.VMEM((tm, tn), F32) for _ in w_list]
    kernel = functools.partial(_mm_kernel, n_a=len(a_list), w_to_a=tuple(w_to_a), cast=cast,
                               n_extra=len(extras), nk=nk, epilogue=epilogue)
    return pl.pallas_call(
        kernel,
        grid_spec=pltpu.PrefetchScalarGridSpec(
            num_scalar_prefetch=2,
            grid=(ncols // tn, m // tm, nk),
            in_specs=in_specs,
            out_specs=pl.BlockSpec((tm, tn), lambda j, i, k, g, n: (i, j)),
            scratch_shapes=scratch),
        out_shape=jax.ShapeDtypeStruct((m, ncols), out_dtype),
        compiler_params=_compiler_params(("arbitrary", "arbitrary", "arbitrary")),
    )(grp, nused, *a_list, *w_list, *extras)


def _epi_plain(parts, e_refs):
    return parts[0]


def _epi_sigmoid(parts, e_refs):
    return _sigmoid(parts[0])


def _epi_gated_sum(parts, e_refs):
    return e_refs[0][...] * parts[0] + e_refs[1][...] * parts[1]


def _epi_residual(parts, e_refs):
    return e_refs[0][...] + parts[0]


def _epi_swiglu(parts, e_refs):
    return _silu(parts[0]) * parts[1]


def _chunk_id(p):
    return jnp.where(p < N_META, 0, 1 + ((p - N_META) >> 6))


def _attn_kernel(slopes_ref, q_ref, k_ref, v_ref, lam_ref, dn_ref, o_ref, *, lambda_init):
    h = pl.program_id(1)
    qb = pl.program_id(2)
    slope = slopes_ref[h]
    n_keys = k_ref.shape[0]
    lp = lam_ref[...]
    s1 = jnp.sum(lp[0:1] * lp[1:2], axis=-1, keepdims=True)
    s2 = jnp.sum(lp[2:3] * lp[3:4], axis=-1, keepdims=True)
    lam = jnp.exp(s1) - jnp.exp(s2) + lambda_init
    qpos = qb * Q_BLOCK + lax.broadcasted_iota(jnp.int32, (Q_BLOCK, n_keys), 0)
    kpos = lax.broadcasted_iota(jnp.int32, (Q_BLOCK, n_keys), 1)
    dist = jnp.abs(qpos - kpos).astype(F32)
    mask = _chunk_id(qpos) >= _chunk_id(kpos)
    bias = slope * dist
    scale = QK_DIM ** -0.5
    probs = []
    for mp in range(2):
        q = q_ref[:, mp * QK_DIM:(mp + 1) * QK_DIM]
        k = k_ref[:, mp * QK_DIM:(mp + 1) * QK_DIM]
        s = lax.dot_general(q, k, (((1,), (1,)), ((), ())), preferred_element_type=F32) * scale
        s = jnp.where(mask, s - bias, NEG_BIG)
        e = jnp.exp(s - jnp.max(s, axis=-1, keepdims=True))
        probs.append(e / jnp.sum(e, axis=-1, keepdims=True))
    a = probs[0] - lam * probs[1]
    o = jnp.dot(a.astype(BF16), v_ref[...], preferred_element_type=F32)
    ms = jnp.mean(o * o, axis=-1, keepdims=True)
    o = o * lax.rsqrt(ms + SUBLN_EPS) * dn_ref[...] * (1.0 - lambda_init)
    o_ref[...] = o.astype(o_ref.dtype)


def diff_attention(qkv, lam_params, diff_norm, slopes, *, bsz, n_pad, lambda_init):
    m = qkv.shape[0]
    nq = n_pad // Q_BLOCK
    kernel = functools.partial(_attn_kernel, lambda_init=lambda_init)
    return pl.pallas_call(
        kernel,
        grid_spec=pltpu.PrefetchScalarGridSpec(
            num_scalar_prefetch=1,
            grid=(bsz, HEADS, nq),
            in_specs=[
                pl.BlockSpec((Q_BLOCK, V_DIM), lambda b, h, q, s: (b * nq + q, h)),
                pl.BlockSpec((n_pad, V_DIM), lambda b, h, q, s: (b, HEADS + h)),
                pl.BlockSpec((n_pad, V_DIM), lambda b, h, q, s: (b, 2 * HEADS + h)),
                pl.BlockSpec((4, QK_DIM), lambda b, h, q, s: (0, 0)),
                pl.BlockSpec((1, V_DIM), lambda b, h, q, s: (0, 0)),
            ],
            out_specs=pl.BlockSpec((Q_BLOCK, V_DIM), lambda b, h, q, s: (b * nq + q, h))),
        out_shape=jax.ShapeDtypeStruct((m, ATTN_WIDTH), BF16),
        compiler_params=_compiler_params(("parallel", "parallel", "arbitrary")),
    )(slopes, qkv, qkv, qkv, lam_params, diff_norm.reshape(1, V_DIM))


def _split3_bf16(x):
    hi = x.astype(BF16)
    r1 = x - hi.astype(F32)
    mid = r1.astype(BF16)
    lo = (r1 - mid.astype(F32)).astype(BF16)
    return hi, mid, lo


def _ssd_kernel(zx_ref, dtr_ref, cw_ref, cb_ref, dtb_ref, alog_ref, dsk_ref, nw_ref, ex_ref, o_ref,
                state_ref, xbuf_ref, y_ref):
    c = pl.program_id(1)
    L = SSD_BLOCK
    halo = 8

    @pl.when(c == 0)
    def _():
        state_ref[...] = jnp.zeros_like(state_ref)
        xbuf_ref[0:halo, :] = jnp.zeros((halo, XBC_WIDTH), F32)

    @pl.when(c > 0)
    def _():
        xbuf_ref[0:halo, :] = xbuf_ref[L:L + halo, :]

    u = zx_ref[:, SSM_WIDTH:SSM_WIDTH + XBC_WIDTH]
    xbuf_ref[halo:halo + L, :] = u
    acc = cb_ref[...] + cw_ref[SSM_CONV - 1:SSM_CONV, :] * u
    for s in range(1, SSM_CONV):
        acc = acc + cw_ref[SSM_CONV - 1 - s:SSM_CONV - s, :] * xbuf_ref[halo - s:halo - s + L, :]
    xbc = _silu(acc)
    xs = xbc[:, :SSM_WIDTH]
    bm = xbc[:, SSM_WIDTH:SSM_WIDTH + SSM_GROUPS * SSM_STATE].astype(BF16)
    cm = xbc[:, SSM_WIDTH + SSM_GROUPS * SSM_STATE:].astype(BF16)

    pre = dtr_ref[...] + dtb_ref[...]
    dt = jnp.maximum(pre, 0.0) + jnp.log(1.0 + jnp.exp(-jnp.abs(pre)))
    adt = dt * (-jnp.exp(alog_ref[...]))
    rows = lax.broadcasted_iota(jnp.int32, (L, LANES), 0)
    cum = adt
    shift = 1
    while shift < L:
        cum = cum + jnp.where(rows >= shift, pltpu.roll(cum, shift, 0), 0.0)
        shift *= 2
    cum_last = cum[L - 1:L, :]
    ecum = jnp.exp(cum)
    edec = jnp.exp(cum_last - cum)
    elast = jnp.broadcast_to(jnp.exp(cum_last), (8, LANES))
    stacked = jnp.concatenate([dt, ecum, edec, elast], axis=0)
    ex = ex_ref[...]
    wide = sum(jnp.dot(p, ex, preferred_element_type=F32) for p in _split3_bf16(stacked))
    dt_w, ecum_w, edec_w, elast_w = wide[0:L], wide[L:2 * L], wide[2 * L:3 * L], wide[3 * L:3 * L + 1]

    xd = xs * dt_w
    xd_b = xd.astype(BF16)
    xdec_b = (xd * edec_w).astype(BF16)
    cum_t = cum.T
    tril = lax.broadcasted_iota(jnp.int32, (L, L), 0) >= lax.broadcasted_iota(jnp.int32, (L, L), 1)
    heads_per_group = SSM_HEADS // SSM_GROUPS
    for g in range(SSM_GROUPS):
        gs = slice(g * SSM_GROUP_WIDTH, (g + 1) * SSM_GROUP_WIDTH)
        bg = bm[:, g * SSM_STATE:(g + 1) * SSM_STATE]
        cg = cm[:, g * SSM_STATE:(g + 1) * SSM_STATE]
        cb = lax.dot_general(cg, bg, (((1,), (1,)), ((), ())), preferred_element_type=F32)
        st = state_ref[:, gs]
        y_off = jnp.dot(cg, st.astype(BF16), preferred_element_type=F32) * ecum_w[:, gs]
        state_ref[:, gs] = elast_w[:, gs] * st + lax.dot_general(
            bg, xdec_b[:, gs], (((0,), (0,)), ((), ())), preferred_element_type=F32)
        y_ref[:, gs] = y_off
        for r in range(heads_per_group):
            hd = g * heads_per_group + r
            hs = slice(hd * SSM_HEAD_DIM, (hd + 1) * SSM_HEAD_DIM)
            seg = cum[:, hd:hd + 1] - cum_t[hd:hd + 1, :]
            lmat = jnp.exp(jnp.where(tril, seg, -jnp.inf))
            y_ref[:, hs] += jnp.dot((cb * lmat).astype(BF16), xd_b[:, hs], preferred_element_type=F32)

    y = y_ref[...] + dsk_ref[...] * xs
    gated = y * _silu(zx_ref[:, 0:SSM_WIDTH])
    for g in range(SSM_GROUPS):
        gs = slice(g * SSM_GROUP_WIDTH, (g + 1) * SSM_GROUP_WIDTH)
        gg = gated[:, gs]
        ms = jnp.mean(gg * gg, axis=-1, keepdims=True)
        o_ref[:, gs] = (gg * lax.rsqrt(ms + NORM_EPS) * nw_ref[:, gs]).astype(o_ref.dtype)


def ssd_mixer(zx, dtr, conv_w, conv_b, dt_bias, a_log, d_skip, ssm_norm, expand, *, bsz, n_pad):
    m = zx.shape[0]
    nc = n_pad // SSD_BLOCK
    L = SSD_BLOCK

    def pad_heads(v):
        return jnp.pad(v.astype(F32), (0, LANES - SSM_HEADS)).reshape(1, LANES)

    def const(shape):
        return pl.BlockSpec(shape, lambda b, c: (0, 0))

    return pl.pallas_call(
        _ssd_kernel,
        grid=(bsz, nc),
        in_specs=[
            pl.BlockSpec((L, ZX_COLS), lambda b, c: (b * nc + c, 0)),
            pl.BlockSpec((L, LANES), lambda b, c: (b * nc + c, 0)),
            const((SSM_CONV, XBC_WIDTH)), const((1, XBC_WIDTH)), const((1, LANES)), const((1, LANES)),
            const((1, SSM_WIDTH)), const((1, SSM_WIDTH)), const((LANES, SSM_WIDTH)),
        ],
        out_specs=pl.BlockSpec((L, SSM_WIDTH), lambda b, c: (b * nc + c, 0)),
        out_shape=jax.ShapeDtypeStruct((m, SSM_WIDTH), BF16),
        scratch_shapes=[pltpu.VMEM((SSM_STATE, SSM_WIDTH), F32), pltpu.VMEM((L + 8, XBC_WIDTH), F32),
                        pltpu.VMEM((L, SSM_WIDTH), F32)],
        compiler_params=_compiler_params(("parallel", "arbitrary")),
    )(zx, dtr, conv_w, conv_b.reshape(1, XBC_WIDTH), pad_heads(dt_bias), pad_heads(a_log),
      jnp.repeat(d_skip.astype(F32), SSM_HEAD_DIM).reshape(1, SSM_WIDTH), ssm_norm.reshape(1, SSM_WIDTH), expand)


def _dispatch_kernel(tok_ref, h_hbm, w_ref, o_ref, buf_ref, sem):
    i = pl.program_id(0)
    g = buf_ref.shape[0]

    def row_copy(r):
        return pltpu.make_async_copy(h_hbm.at[pl.ds(tok_ref[i * g + r], 1)], buf_ref.at[pl.ds(r, 1)], sem)

    def start(r, carry):
        row_copy(r).start()
        return carry

    def wait(r, carry):
        row_copy(r).wait()
        return carry

    lax.fori_loop(0, g, start, 0)
    lax.fori_loop(0, g, wait, 0)
    x = buf_ref[...]
    ms = jnp.mean(x * x, axis=-1, keepdims=True)
    o_ref[...] = (x * lax.rsqrt(ms + NORM_EPS) * w_ref[...]).astype(o_ref.dtype)


def moe_dispatch(h, w, tok, *, tg=MOE_ROW_TILE):
    d = h.shape[1]
    rows = tok.shape[0]
    return pl.pallas_call(
        _dispatch_kernel,
        grid_spec=pltpu.PrefetchScalarGridSpec(
            num_scalar_prefetch=1,
            grid=(rows // tg,),
            in_specs=[pl.BlockSpec(memory_space=pl.ANY), pl.BlockSpec((1, d), lambda i, t: (0, 0))],
            out_specs=pl.BlockSpec((tg, d), lambda i, t: (i, 0)),
            scratch_shapes=[pltpu.VMEM((tg, d), F32), pltpu.SemaphoreType.DMA(())]),
        out_shape=jax.ShapeDtypeStruct((rows, d), BF16),
        compiler_params=_compiler_params(("arbitrary",)),
    )(tok, h, w.reshape(1, d))


def _combine_kernel(pos1_ref, pos2_ref, y_hbm, h_hbm, route_hbm, w_ref, o_ref,
                    y1_ref, y2_ref, h_ref, r_ref, sem_y, sem_h, *, n_pad, n_frames):
    i = pl.program_id(0)
    tr = h_ref.shape[0]
    tiles_per_batch = n_frames // tr
    b = i // tiles_per_batch
    row0 = b * n_pad + N_META + (i - b * tiles_per_batch) * tr
    h_copy = pltpu.make_async_copy(h_hbm.at[pl.ds(row0, tr)], h_ref, sem_h.at[0])
    r_copy = pltpu.make_async_copy(route_hbm.at[pl.ds(row0, tr)], r_ref, sem_h.at[1])
    h_copy.start()
    r_copy.start()

    def y_copies(r):
        t = i * tr + r
        return (pltpu.make_async_copy(y_hbm.at[pl.ds(pos1_ref[t], 1)], y1_ref.at[pl.ds(r, 1)], sem_y),
                pltpu.make_async_copy(y_hbm.at[pl.ds(pos2_ref[t], 1)], y2_ref.at[pl.ds(r, 1)], sem_y))

    def start(r, carry):
        for cp in y_copies(r):
            cp.start()
        return carry

    def wait(r, carry):
        for cp in y_copies(r):
            cp.wait()
        return carry

    lax.fori_loop(0, tr, start, 0)
    h_copy.wait()
    r_copy.wait()
    lax.fori_loop(0, tr, wait, 0)
    route = r_ref[...]
    lane = lax.broadcasted_iota(jnp.int32, route.shape, 1).astype(F32)
    in_experts = lane < N_EXPERTS
    w1 = jnp.sum(jnp.where(jnp.logical_and(in_experts, lane == route[:, N_EXPERTS:N_EXPERTS + 1]), route, 0.0),
                 axis=-1, keepdims=True)
    w2 = jnp.sum(jnp.where(jnp.logical_and(in_experts, lane == route[:, N_EXPERTS + 1:N_EXPERTS + 2]), route, 0.0),
                 axis=-1, keepdims=True)
    x = h_ref[...] + w1 * y1_ref[...] + w2 * y2_ref[...]
    ms = jnp.mean(x * x, axis=-1, keepdims=True)
    o_ref[...] = x * lax.rsqrt(ms + NORM_EPS) * w_ref[...]


def moe_combine(ys, h, route, final_norm, pos1, pos2, *, bsz, n_pad, n_frames, tr=MOE_ROW_TILE):
    d = h.shape[1]
    kernel = functools.partial(_combine_kernel, n_pad=n_pad, n_frames=n_frames)
    any_spec = pl.BlockSpec(memory_space=pl.ANY)
    return pl.pallas_call(
        kernel,
        grid_spec=pltpu.PrefetchScalarGridSpec(
            num_scalar_prefetch=2,
            grid=(bsz * n_frames // tr,),
            in_specs=[any_spec, any_spec, any_spec, pl.BlockSpec((1, d), lambda i, p1, p2: (0, 0))],
            out_specs=pl.BlockSpec((tr, d), lambda i, p1, p2: (i, 0)),
            scratch_shapes=[pltpu.VMEM((tr, d), F32), pltpu.VMEM((tr, d), F32), pltpu.VMEM((tr, d), F32),
                            pltpu.VMEM((tr, LANES), F32), pltpu.SemaphoreType.DMA(()),
                            pltpu.SemaphoreType.DMA((2,))]),
        out_shape=jax.ShapeDtypeStruct((bsz * n_frames, d), F32),
        compiler_params=_compiler_params(("arbitrary",)),
    )(pos1, pos2, ys, h, route, final_norm.reshape(1, d))


def _moe_plan(route, *, bsz, n_pad, n_frames, tm):
    n_tok = bsz * n_frames
    cap = 2 * n_tok + N_EXPERTS * tm
    frames = route.reshape(bsz, n_pad, LANES)[:, N_META:N_META + n_frames].reshape(n_tok, LANES)
    idx = frames[:, N_EXPERTS:N_EXPERTS + 2].astype(jnp.int32)
    sel = (idx[:, :, None] == jnp.arange(N_EXPERTS)[None, None, :]).any(axis=1)
    counts = sel.sum(axis=0)
    padded = ((counts + tm - 1) // tm) * tm
    ends = jnp.cumsum(padded)
    starts = ends - padded
    rank = jnp.cumsum(sel, axis=0) - 1
    pos = starts[None, :] + rank
    pos12 = jnp.take_along_axis(pos, idx, axis=1).astype(jnp.int32)
    tok_rows = (jnp.arange(n_tok) // n_frames) * n_pad + N_META + jnp.arange(n_tok) % n_frames
    tok = jnp.zeros((cap,), jnp.int32).at[pos12.reshape(-1)].set(
        jnp.repeat(tok_rows.astype(jnp.int32), 2), mode="drop")
    tile_start = jnp.arange(cap // tm) * tm
    tile_grp = jnp.minimum(jnp.searchsorted(ends, tile_start, side="right"), N_EXPERTS - 1).astype(jnp.int32)
    nused = (ends[-1] // tm).astype(jnp.int32).reshape(1)
    return tok, tile_grp, nused, pos12[:, 0], pos12[:, 1]


def _dense_plan(m, tm, group):
    return jnp.full((m // tm,), group, jnp.int32), jnp.full((1,), m // tm, jnp.int32)


def mixer_params(mix_w_in):
    return {
        "slopes": jnp.exp2(-8.0 * (jnp.arange(HEADS, dtype=F32) + 1.0) / HEADS),
        "expand": (jnp.arange(LANES)[:, None] == (jnp.arange(SSM_WIDTH) // SSM_HEAD_DIM)[None, :]).astype(BF16),
        "w_dt": jnp.pad(mix_w_in[:, :, DT_COL0:GATE_COL0], ((0, 0), (0, 0), (0, LANES - SSM_HEADS))),
        "w_gates": mix_w_in[:, :, GATE_COL0:],
    }


def mixer_layer(h, layer, p, *, bsz, n_pad, tm):
    m, d = h.shape
    lambda_init = 0.8 - 0.6 * math.exp(-0.3 * layer)
    grp, nused = _dense_plan(m, tm, layer)
    mm = functools.partial(grouped_matmul, grp=grp, nused=nused, tm=tm)
    hn = rmsnorm(h, p["mix_norm"][layer])
    w_in = p["mix_w_in"]
    qkv = mm([hn], [w_in], [0], [], _epi_plain, ncols=QKV_COLS, tn=512, out_dtype=BF16)
    zx = mm([hn], [w_in], [0], [], _epi_plain, ncols=ZX_COLS, tn=512, out_dtype=F32, w_col0=ZX_COL0)
    dtr = mm([hn], [p["w_dt"]], [0], [], _epi_plain, ncols=LANES, tn=LANES, out_dtype=F32)
    gates = mm([hn], [p["w_gates"]], [0], [], _epi_sigmoid, ncols=2 * d, tn=512, out_dtype=F32)
    lam_params = jnp.stack([p["lambda_q1"][layer], p["lambda_k1"][layer], p["lambda_q2"][layer],
                            p["lambda_k2"][layer]]).astype(F32)
    o_attn = diff_attention(qkv, lam_params, p["diff_norm"][layer], p["slopes"], bsz=bsz, n_pad=n_pad,
                            lambda_init=lambda_init)
    o_ssm = ssd_mixer(zx, dtr, p["conv_w"][layer], p["conv_b"][layer], p["dt_bias"][layer], p["a_log"][layer],
                      p["d_skip"][layer], p["ssm_norm"][layer], p["expand"], bsz=bsz, n_pad=n_pad)
    merged = mm([o_attn, o_ssm], [p["w_proj_attn"], p["w_proj_ssm"]], [0, 1], [gates, gates], _epi_gated_sum,
                ncols=d, tn=256, out_dtype=BF16, extra_col0=(0, d))
    return mm([merged], [p["w_out"]], [0], [h], _epi_residual, ncols=d, tn=512, out_dtype=F32, extra_col0=(0,))


def dense_ffn(h, norm_w, w_gate, w_up, w_down, group, *, tm):
    m, d = h.shape
    d_ff = w_gate.shape[2]
    grp, nused = _dense_plan(m, tm, group)
    mm = functools.partial(grouped_matmul, grp=grp, nused=nused, tm=tm)
    hn = rmsnorm(h, norm_w)
    u = mm([hn], [w_gate, w_up], [0, 0], [], _epi_swiglu, ncols=d_ff, tn=256, out_dtype=BF16)
    return mm([u], [w_down], [0], [h], _epi_residual, ncols=d, tn=256, out_dtype=F32, extra_col0=(0,), tk=d_ff // 2)


def moe_ffn_final(h, norm_w, router_w, w_gate, w_up, w_down, final_norm, group, *, bsz, n_pad, n_frames):
    d = h.shape[1]
    d_fe = w_gate.shape[3]
    r_pad = jnp.pad(router_w.astype(F32), ((0, 0), (0, LANES - N_EXPERTS)))
    route = route_tokens(h, norm_w, r_pad)
    tok, tile_grp, tiles_used, pos1, pos2 = _moe_plan(route, bsz=bsz, n_pad=n_pad, n_frames=n_frames,
                                                       tm=MOE_ROW_TILE)
    xs = moe_dispatch(h, norm_w, tok)
    mm = functools.partial(grouped_matmul, grp=tile_grp + group * N_EXPERTS, nused=tiles_used, tm=MOE_ROW_TILE)
    u = mm([xs], [w_gate.reshape(-1, d, d_fe), w_up.reshape(-1, d, d_fe)], [0, 0], [], _epi_swiglu,
           ncols=d_fe, tn=256, out_dtype=BF16)
    ys = mm([u], [w_down.reshape(-1, d_fe, d)], [0], [], _epi_plain, ncols=d, tn=512, out_dtype=F32)
    return moe_combine(ys, h, route, final_norm, pos1, pos2, bsz=bsz, n_pad=n_pad, n_frames=n_frames)


def kernel(x, meta_tokens, mix_norm, mix_w_in, conv_w, conv_b, dt_bias, a_log, d_skip, ssm_norm, lambda_q1, lambda_k1, lambda_q2, lambda_k2, diff_norm, w_proj_attn, w_proj_ssm, w_out, ffn_norm, ffn_w_gate, ffn_w_up, ffn_w_down, router, moe_w_gate, moe_w_up, moe_w_down, final_norm):
    bsz, n_frames, d = x.shape
    assert mix_norm.shape[0] == 2 and d == D_MODEL, "layer 0: dense FFN; layer 1: MoE FFN, then the final norm"
    n = N_META + n_frames
    n_pad = -(-n // Q_BLOCK) * Q_BLOCK
    m = bsz * n_pad
    tm = n_pad // 2
    assert tm % 16 == 0 and n_pad % SSD_BLOCK == 0 and n_frames % MOE_ROW_TILE == 0

    meta = jnp.broadcast_to(meta_tokens[None].astype(x.dtype), (bsz, N_META, d))
    h = jnp.concatenate([meta, x, jnp.zeros((bsz, n_pad - n, d), x.dtype)], axis=1).reshape(m, d)

    p = dict(mix_norm=mix_norm, mix_w_in=mix_w_in, conv_w=conv_w, conv_b=conv_b, dt_bias=dt_bias, a_log=a_log,
             d_skip=d_skip, ssm_norm=ssm_norm, lambda_q1=lambda_q1, lambda_k1=lambda_k1, lambda_q2=lambda_q2,
             lambda_k2=lambda_k2, diff_norm=diff_norm, w_proj_attn=w_proj_attn, w_proj_ssm=w_proj_ssm, w_out=w_out)
    p.update(mixer_params(mix_w_in))
    h = mixer_layer(h, 0, p, bsz=bsz, n_pad=n_pad, tm=tm)
    h = dense_ffn(h, ffn_norm[0], ffn_w_gate, ffn_w_up, ffn_w_down, 0, tm=tm)
    h = mixer_layer(h, 1, p, bsz=bsz, n_pad=n_pad, tm=tm)
    out = moe_ffn_final(h, ffn_norm[1], router[0], moe_w_gate, moe_w_up, moe_w_down, final_norm, 0,
                        bsz=bsz, n_pad=n_pad, n_frames=n_frames)
    return out.reshape(bsz, n_frames, d)
```

```python
import functools
import math

import jax
import jax.numpy as jnp
from jax import lax
from jax.experimental import pallas as pl
from jax.experimental.pallas import tpu as pltpu

F32 = jnp.float32
BF16 = jnp.bfloat16

LANES = 128
VMEM_LIMIT_BYTES_V7X = 56 * 2**20

D_MODEL = 4096
N_META = 16
CHUNK = 64
Q_BLOCK = 128
HEADS = 8
QK_DIM = 128
V_DIM = 256
ATTN_WIDTH = HEADS * V_DIM
SUBLN_EPS = 1e-5
SSM_WIDTH = 2048
SSM_HEAD_DIM = 64
SSM_HEADS = 32
SSM_GROUPS = 4
SSM_STATE = 128
SSM_CONV = 4
SSM_GROUP_WIDTH = SSM_WIDTH // SSM_GROUPS
XBC_WIDTH = SSM_WIDTH + 2 * SSM_GROUPS * SSM_STATE
QKV_COLS = 3 * ATTN_WIDTH
ZX_COL0 = QKV_COLS
ZX_COLS = SSM_WIDTH + XBC_WIDTH
DT_COL0 = ZX_COL0 + ZX_COLS
GATE_COL0 = DT_COL0 + SSM_HEADS
N_EXPERTS = 8
NORM_EPS = 1e-6
NEG_BIG = -1e30
SSD_BLOCK = 128

ROW_TILE = 1088
MOE_ROW_TILE = 256
NORM_ROW_TILE = 272


def _silu(x):
    return x * (1.0 / (1.0 + jnp.exp(-x)))


def _sigmoid(x):
    return 1.0 / (1.0 + jnp.exp(-x))


def _compiler_params(semantics):
    return pltpu.CompilerParams(dimension_semantics=semantics, vmem_limit_bytes=VMEM_LIMIT_BYTES_V7X)


def _rmsnorm_kernel(x_ref, w_ref, o_ref, *, eps):
    x = x_ref[...]
    ms = jnp.mean(x * x, axis=-1, keepdims=True)
    o_ref[...] = (x * lax.rsqrt(ms + eps) * w_ref[...]).astype(o_ref.dtype)


def _norm_row_tile(m):
    return NORM_ROW_TILE if m % NORM_ROW_TILE == 0 else Q_BLOCK


def rmsnorm(x, w, *, eps=NORM_EPS, out_dtype=BF16):
    m, d = x.shape
    tr = _norm_row_tile(m)
    return pl.pallas_call(
        functools.partial(_rmsnorm_kernel, eps=eps),
        grid=(m // tr,),
        in_specs=[pl.BlockSpec((tr, d), lambda i: (i, 0)), pl.BlockSpec((1, d), lambda i: (0, 0))],
        out_specs=pl.BlockSpec((tr, d), lambda i: (i, 0)),
        out_shape=jax.ShapeDtypeStruct((m, d), out_dtype),
        compiler_params=_compiler_params(("parallel",)),
    )(x, w.reshape(1, d))


def _split_bf16(x):
    hi = x.astype(BF16)
    lo = (x - hi.astype(F32)).astype(BF16)
    return hi, lo


def _router_kernel(x_ref, w_ref, r_ref, o_ref, *, eps):
    x = x_ref[...]
    ms = jnp.mean(x * x, axis=-1, keepdims=True)
    hn = x * lax.rsqrt(ms + eps) * w_ref[...]
    h_hi, h_lo = _split_bf16(hn)
    r_hi, r_lo = _split_bf16(r_ref[...])
    logits = (jnp.dot(h_hi, r_hi, preferred_element_type=F32)
              + jnp.dot(h_hi, r_lo, preferred_element_type=F32)
              + jnp.dot(h_lo, r_hi, preferred_element_type=F32))
    lane = lax.broadcasted_iota(jnp.int32, logits.shape, 1).astype(F32)
    lg = jnp.where(lane < N_EXPERTS, logits, -jnp.inf)
    m1 = jnp.max(lg, axis=-1, keepdims=True)
    i1 = jnp.min(jnp.where(lg == m1, lane, float(LANES)), axis=-1, keepdims=True)
    lg2 = jnp.where(lane == i1, -jnp.inf, lg)
    m2 = jnp.max(lg2, axis=-1, keepdims=True)
    i2 = jnp.min(jnp.where(lg2 == m2, lane, float(LANES)), axis=-1, keepdims=True)
    e2 = jnp.exp(m2 - m1)
    den = 1.0 + e2
    out = jnp.where(lane == i1, 1.0 / den, 0.0) + jnp.where(lane == i2, e2 / den, 0.0)
    out = jnp.where(lane == N_EXPERTS, i1, out)
    out = jnp.where(lane == N_EXPERTS + 1, i2, out)
    o_ref[...] = out


def route_tokens(x, w, r_pad):
    m, d = x.shape
    tr = _norm_row_tile(m)
    return pl.pallas_call(
        functools.partial(_router_kernel, eps=NORM_EPS),
        grid=(m // tr,),
        in_specs=[pl.BlockSpec((tr, d), lambda i: (i, 0)), pl.BlockSpec((1, d), lambda i: (0, 0)),
                  pl.BlockSpec((d, LANES), lambda i: (0, 0))],
        out_specs=pl.BlockSpec((tr, LANES), lambda i: (i, 0)),
        out_shape=jax.ShapeDtypeStruct((m, LANES), F32),
        compiler_params=_compiler_params(("parallel",)),
    )(x, w.reshape(1, d), r_pad)


def _mm_kernel(grp_ref, nused_ref, *refs, n_a, w_to_a, cast, w_shift, n_extra, nk, epilogue):
    n_w = len(w_to_a)
    a_refs = refs[:n_a]
    w_refs = refs[n_a:n_a + n_w]
    refs = list(refs[n_a + n_w:])
    wnext_refs = [refs.pop(0) if s else None for s in w_shift]
    e_refs = [refs.pop(0) for _ in range(n_extra)]
    o_ref = refs.pop(0)
    scratch = refs
    wb_refs = [scratch.pop(0) if c else None for c in cast]
    acc_refs = [scratch.pop(0) for _ in range(n_w)] if nk > 1 else []
    i = pl.program_id(1)
    k = pl.program_id(2)

    @pl.when(i >= nused_ref[0])
    def _():
        o_ref[...] = jnp.zeros_like(o_ref)

    @pl.when(i < nused_ref[0])
    def _():
        if any(cast):
            def do_cast():
                for w_ref, wn_ref, wb_ref, s in zip(w_refs, wnext_refs, wb_refs, w_shift):
                    if wb_ref is None:
                        continue
                    if s:
                        tn = w_ref.shape[1]
                        wide = jnp.concatenate([w_ref[...], wn_ref[...]], axis=1)
                        wb_ref[...] = wide[:, s:s + tn].astype(BF16)
                    else:
                        wb_ref[...] = w_ref[...].astype(BF16)
            if nk == 1:
                changed = jnp.logical_or(i == 0, grp_ref[i] != grp_ref[jnp.maximum(i - 1, 0)])
                pl.when(changed)(do_cast)
            else:
                do_cast()
        parts = []
        for wi in range(n_w):
            w = wb_refs[wi][...] if cast[wi] else w_refs[wi][...]
            parts.append(jnp.dot(a_refs[w_to_a[wi]][...], w, preferred_element_type=F32))
        if nk == 1:
            o_ref[...] = epilogue(parts, e_refs).astype(o_ref.dtype)
        else:
            @pl.when(k == 0)
            def _():
                for acc, p in zip(acc_refs, parts):
                    acc[...] = p

            @pl.when(k > 0)
            def _():
                for acc, p in zip(acc_refs, parts):
                    acc[...] += p

            @pl.when(k == nk - 1)
            def _():
                o_ref[...] = epilogue([acc[...] for acc in acc_refs], e_refs).astype(o_ref.dtype)


def grouped_matmul(a_list, w_list, w_to_a, extras, epilogue, *, grp, nused, ncols, tm, tn, out_dtype,
                   w_col0=None, w_shift=None, extra_col0=(), tk=None):
    m, kdim = a_list[0].shape
    tk = kdim if tk is None else tk
    nk = kdim // tk
    w_col0 = (0,) * len(w_list) if w_col0 is None else tuple(w_col0)
    w_shift = (0,) * len(w_list) if w_shift is None else tuple(w_shift)
    assert m % tm == 0 and ncols % tn == 0 and kdim % tk == 0 and all(c % tn == 0 for c in w_col0)
    assert all(a.shape == (m, kdim) for a in a_list) and all(w.shape[1] == kdim for w in w_list)
    cast = tuple(w.dtype != BF16 for w in w_list)
    assert all(c or not s for c, s in zip(cast, w_shift)) and all(0 <= s < LANES for s in w_shift)
    ecb = tuple(c // tn for c in extra_col0)
    lane_tiles = tn // LANES

    def row(i, n):
        return jnp.minimum(i, n[0] - 1)

    def w_spec(cb):
        return pl.BlockSpec((None, tk, tn), lambda j, i, k, g, n: (g[row(i, n)], k, cb + j))

    def w_next_spec(cb):
        return pl.BlockSpec((None, tk, LANES), lambda j, i, k, g, n: (g[row(i, n)], k, (cb + j + 1) * lane_tiles))

    in_specs = [pl.BlockSpec((tm, tk), lambda j, i, k, g, n: (row(i, n), k)) for _ in a_list]
    in_specs += [w_spec(c // tn) for c in w_col0]
    in_specs += [w_next_spec(c // tn) for c, s in zip(w_col0, w_shift) if s]
    in_specs += [pl.BlockSpec((tm, tn), functools.partial(lambda j, i, k, g, n, eb: (row(i, n), eb + j), eb=eb))
                 for eb in ecb]
    scratch = [pltpu.VMEM((tk, tn), BF16) for c in cast if c]
    if nk > 1:
        scratch += [pltpu.VMEM((tm, tn), F32) for _ in w_list]
    kernel = functools.partial(_mm_kernel, n_a=len(a_list), w_to_a=tuple(w_to_a), cast=cast, w_shift=w_shift,
                               n_extra=len(extras), nk=nk, epilogue=epilogue)
    w_next = [w for w, s in zip(w_list, w_shift) if s]
    return pl.pallas_call(
        kernel,
        grid_spec=pltpu.PrefetchScalarGridSpec(
            num_scalar_prefetch=2,
            grid=(ncols // tn, m // tm, nk),
            in_specs=in_specs,
            out_specs=pl.BlockSpec((tm, tn), lambda j, i, k, g, n: (i, j)),
            scratch_shapes=scratch),
        out_shape=jax.ShapeDtypeStruct((m, ncols), out_dtype),
        compiler_params=_compiler_params(("arbitrary", "arbitrary", "arbitrary")),
    )(grp, nused, *a_list, *w_list, *w_next, *extras)


def _epi_plain(parts, e_refs):
    return parts[0]


def _epi_sigmoid(parts, e_refs):
    return _sigmoid(parts[0])


def _epi_gated_sum(parts, e_refs):
    return e_refs[0][...] * parts[0] + e_refs[1][...] * parts[1]


def _epi_residual(parts, e_refs):
    return e_refs[0][...] + parts[0]


def _epi_swiglu(parts, e_refs):
    return _silu(parts[0]) * parts[1]


def attention_tables(n_pad):
    slopes = jnp.exp2(-8.0 * (jnp.arange(HEADS, dtype=F32) + 1.0) / HEADS)
    i = jnp.arange(Q_BLOCK)[:, None]
    j = jnp.arange(2 * Q_BLOCK)[None, :]
    visible = (i - N_META) // CHUNK >= (j - N_META) // CHUNK
    rel = (i - jnp.abs(i - j)).astype(F32)
    diag = jnp.where(visible[None], slopes[:, None, None] * rel[None], NEG_BIG)
    col = slopes[:, None, None] * jnp.arange(n_pad, dtype=F32)[None, None, :]
    return diag, col


def _attn_kernel(q_ref, k_ref, v_ref, lam_ref, dn_ref, diag_ref, col_ref, o_ref, kt_ref, *, lambda_init):
    n_pad = q_ref.shape[0]
    lp = lam_ref[...]
    s1 = jnp.sum(lp[0:1] * lp[1:2], axis=-1, keepdims=True)
    s2 = jnp.sum(lp[2:3] * lp[3:4], axis=-1, keepdims=True)
    lam = jnp.exp(s1) - jnp.exp(s2) + lambda_init
    for mp in range(2):
        kt_ref[mp] = k_ref[:, mp * QK_DIM:(mp + 1) * QK_DIM].T
    scale = QK_DIM ** -0.5
    gain = dn_ref[...] * (1.0 - lambda_init)
    for qb in range(n_pad // Q_BLOCK):
        q0 = qb * Q_BLOCK
        width = min(2 * Q_BLOCK, n_pad - q0)
        rows = slice(q0, q0 + Q_BLOCK)
        diag_bias = diag_ref[:, :width]
        past_bias = col_ref[:, :q0] - col_ref[:, q0:q0 + 1] if qb else None
        a_diag = None
        a_past = None
        for mp in range(2):
            q = q_ref[rows, mp * QK_DIM:(mp + 1) * QK_DIM]
            s_d = jnp.dot(q, kt_ref[mp, :, q0:q0 + width], preferred_element_type=F32) * scale + diag_bias
            mx = jnp.max(s_d, axis=-1, keepdims=True)
            if qb:
                s_p = jnp.dot(q, kt_ref[mp, :, :q0], preferred_element_type=F32) * scale + past_bias
                mx = jnp.maximum(mx, jnp.max(s_p, axis=-1, keepdims=True))
                e_p = jnp.exp(s_p - mx)
            e_d = jnp.exp(s_d - mx)
            den = jnp.sum(e_d, axis=-1, keepdims=True)
            if qb:
                den = den + jnp.sum(e_p, axis=-1, keepdims=True)
            coef = 1.0 / den if mp == 0 else -lam / den
            a_diag = e_d * coef if mp == 0 else a_diag + e_d * coef
            if qb:
                a_past = e_p * coef if mp == 0 else a_past + e_p * coef
        o = jnp.dot(a_diag.astype(BF16), v_ref[q0:q0 + width, :], preferred_element_type=F32)
        if qb:
            o = o + jnp.dot(a_past.astype(BF16), v_ref[:q0, :], preferred_element_type=F32)
        ms = jnp.mean(o * o, axis=-1, keepdims=True)
        o_ref[rows, :] = (o * lax.rsqrt(ms + SUBLN_EPS) * gain).astype(o_ref.dtype)


def diff_attention(qkv, lam_params, diff_norm, diag, col, *, bsz, n_pad, lambda_init):
    m = qkv.shape[0]
    kernel = functools.partial(_attn_kernel, lambda_init=lambda_init)
    return pl.pallas_call(
        kernel,
        grid=(bsz, HEADS),
        in_specs=[
            pl.BlockSpec((n_pad, V_DIM), lambda b, h: (b, h)),
            pl.BlockSpec((n_pad, V_DIM), lambda b, h: (b, HEADS + h)),
            pl.BlockSpec((n_pad, V_DIM), lambda b, h: (b, 2 * HEADS + h)),
            pl.BlockSpec((4, QK_DIM), lambda b, h: (0, 0)),
            pl.BlockSpec((1, V_DIM), lambda b, h: (0, 0)),
            pl.BlockSpec((None, Q_BLOCK, 2 * Q_BLOCK), lambda b, h: (h, 0, 0)),
            pl.BlockSpec((None, 1, n_pad), lambda b, h: (h, 0, 0)),
        ],
        out_specs=pl.BlockSpec((n_pad, V_DIM), lambda b, h: (b, h)),
        out_shape=jax.ShapeDtypeStruct((m, ATTN_WIDTH), BF16),
        scratch_shapes=[pltpu.VMEM((2, QK_DIM, n_pad), BF16)],
        compiler_params=_compiler_params(("parallel", "parallel")),
    )(qkv, qkv, qkv, lam_params, diff_norm.reshape(1, V_DIM), diag, col)


def _split3_bf16(x):
    hi = x.astype(BF16)
    r1 = x - hi.astype(F32)
    mid = r1.astype(BF16)
    lo = (r1 - mid.astype(F32)).astype(BF16)
    return hi, mid, lo


def _ssd_kernel(zx_ref, dtr_ref, cw_ref, cb_ref, dtb_ref, alog_ref, dsk_ref, nw_ref, ex_ref, o_ref,
                state_ref, xbuf_ref, y_ref):
    c = pl.program_id(1)
    L = SSD_BLOCK
    halo = 8

    @pl.when(c == 0)
    def _():
        state_ref[...] = jnp.zeros_like(state_ref)
        xbuf_ref[0:halo, :] = jnp.zeros((halo, XBC_WIDTH), F32)

    @pl.when(c > 0)
    def _():
        xbuf_ref[0:halo, :] = xbuf_ref[L:L + halo, :]

    u = zx_ref[:, SSM_WIDTH:SSM_WIDTH + XBC_WIDTH]
    xbuf_ref[halo:halo + L, :] = u
    acc = cb_ref[...] + cw_ref[SSM_CONV - 1:SSM_CONV, :] * u
    for s in range(1, SSM_CONV):
        acc = acc + cw_ref[SSM_CONV - 1 - s:SSM_CONV - s, :] * xbuf_ref[halo - s:halo - s + L, :]
    xbc = _silu(acc)
    xs = xbc[:, :SSM_WIDTH]
    bm = xbc[:, SSM_WIDTH:SSM_WIDTH + SSM_GROUPS * SSM_STATE].astype(BF16)
    cm = xbc[:, SSM_WIDTH + SSM_GROUPS * SSM_STATE:].astype(BF16)

    pre = dtr_ref[...] + dtb_ref[...]
    dt = jnp.maximum(pre, 0.0) + jnp.log(1.0 + jnp.exp(-jnp.abs(pre)))
    adt = dt * (-jnp.exp(alog_ref[...]))
    rows = lax.broadcasted_iota(jnp.int32, (L, LANES), 0)
    cum = adt
    shift = 1
    while shift < L:
        cum = cum + jnp.where(rows >= shift, pltpu.roll(cum, shift, 0), 0.0)
        shift *= 2
    cum_last = cum[L - 1:L, :]
    ecum = jnp.exp(cum)
    edec = jnp.exp(cum_last - cum)
    elast = jnp.broadcast_to(jnp.exp(cum_last), (8, LANES))
    stacked = jnp.concatenate([dt, ecum, edec, elast], axis=0)
    ex = ex_ref[...]
    wide = sum(jnp.dot(p, ex, preferred_element_type=F32) for p in _split3_bf16(stacked))
    dt_w, ecum_w, edec_w, elast_w = wide[0:L], wide[L:2 * L], wide[2 * L:3 * L], wide[3 * L:3 * L + 1]

    xd = xs * dt_w
    xd_b = xd.astype(BF16)
    xdec_b = (xd * edec_w).astype(BF16)
    cum_t = cum.T
    tril = lax.broadcasted_iota(jnp.int32, (L, L), 0) >= lax.broadcasted_iota(jnp.int32, (L, L), 1)
    heads_per_group = SSM_HEADS // SSM_GROUPS
    for g in range(SSM_GROUPS):
        gs = slice(g * SSM_GROUP_WIDTH, (g + 1) * SSM_GROUP_WIDTH)
        bg = bm[:, g * SSM_STATE:(g + 1) * SSM_STATE]
        cg = cm[:, g * SSM_STATE:(g + 1) * SSM_STATE]
        cb = lax.dot_general(cg, bg, (((1,), (1,)), ((), ())), preferred_element_type=F32)
        st = state_ref[:, gs]
        y_off = jnp.dot(cg, st.astype(BF16), preferred_element_type=F32) * ecum_w[:, gs]
        state_ref[:, gs] = elast_w[:, gs] * st + lax.dot_general(
            bg, xdec_b[:, gs], (((0,), (0,)), ((), ())), preferred_element_type=F32)
        y_ref[:, gs] = y_off
        for r in range(heads_per_group):
            hd = g * heads_per_group + r
            hs = slice(hd * SSM_HEAD_DIM, (hd + 1) * SSM_HEAD_DIM)
            seg = cum[:, hd:hd + 1] - cum_t[hd:hd + 1, :]
            lmat = jnp.exp(jnp.where(tril, seg, -jnp.inf))
            y_ref[:, hs] += jnp.dot((cb * lmat).astype(BF16), xd_b[:, hs], preferred_element_type=F32)

    y = y_ref[...] + dsk_ref[...] * xs
    gated = y * _silu(zx_ref[:, 0:SSM_WIDTH])
    for g in range(SSM_GROUPS):
        gs = slice(g * SSM_GROUP_WIDTH, (g + 1) * SSM_GROUP_WIDTH)
        gg = gated[:, gs]
        ms = jnp.mean(gg * gg, axis=-1, keepdims=True)
        o_ref[:, gs] = (gg * lax.rsqrt(ms + NORM_EPS) * nw_ref[:, gs]).astype(o_ref.dtype)


def ssd_mixer(zx, dtr, conv_w, conv_b, dt_bias, a_log, d_skip, ssm_norm, expand, *, bsz, n_pad):
    m = zx.shape[0]
    nc = n_pad // SSD_BLOCK
    L = SSD_BLOCK

    def pad_heads(v):
        return jnp.pad(v.astype(F32), (0, LANES - SSM_HEADS)).reshape(1, LANES)

    def const(shape):
        return pl.BlockSpec(shape, lambda b, c: (0, 0))

    return pl.pallas_call(
        _ssd_kernel,
        grid=(bsz, nc),
        in_specs=[
            pl.BlockSpec((L, ZX_COLS), lambda b, c: (b * nc + c, 0)),
            pl.BlockSpec((L, LANES), lambda b, c: (b * nc + c, 0)),
            const((SSM_CONV, XBC_WIDTH)), const((1, XBC_WIDTH)), const((1, LANES)), const((1, LANES)),
            const((1, SSM_WIDTH)), const((1, SSM_WIDTH)), const((LANES, SSM_WIDTH)),
        ],
        out_specs=pl.BlockSpec((L, SSM_WIDTH), lambda b, c: (b * nc + c, 0)),
        out_shape=jax.ShapeDtypeStruct((m, SSM_WIDTH), BF16),
        scratch_shapes=[pltpu.VMEM((SSM_STATE, SSM_WIDTH), F32), pltpu.VMEM((L + 8, XBC_WIDTH), F32),
                        pltpu.VMEM((L, SSM_WIDTH), F32)],
        compiler_params=_compiler_params(("parallel", "arbitrary")),
    )(zx, dtr, conv_w, conv_b.reshape(1, XBC_WIDTH), pad_heads(dt_bias), pad_heads(a_log),
      jnp.repeat(d_skip.astype(F32), SSM_HEAD_DIM).reshape(1, SSM_WIDTH), ssm_norm.reshape(1, SSM_WIDTH), expand)


def _dispatch_kernel(tok_ref, nused_ref, h_hbm, w_ref, o_ref, buf_ref, sem):
    i = pl.program_id(0)
    g = buf_ref.shape[1]
    nused = nused_ref[0]
    slot = i % 2

    def row_copy(src_row, dst_slot, r):
        return pltpu.make_async_copy(h_hbm.at[pl.ds(src_row, 1)], buf_ref.at[dst_slot, pl.ds(r, 1)],
                                     sem.at[dst_slot])

    def start_tile(tile, dst_slot):
        def body(r, carry):
            row_copy(tok_ref[tile * g + r], dst_slot, r).start()
            return carry
        lax.fori_loop(0, g, body, 0, unroll=8)

    def wait_tile(dst_slot):
        def body(r, carry):
            row_copy(0, dst_slot, r).wait()
            return carry
        lax.fori_loop(0, g, body, 0, unroll=8)

    @pl.when(jnp.logical_and(i == 0, nused > 0))
    def _():
        start_tile(0, 0)

    @pl.when(i + 1 < nused)
    def _():
        start_tile(i + 1, 1 - slot)

    @pl.when(i < nused)
    def _():
        wait_tile(slot)
        x = buf_ref[slot]
        ms = jnp.mean(x * x, axis=-1, keepdims=True)
        o_ref[...] = (x * lax.rsqrt(ms + NORM_EPS) * w_ref[...]).astype(o_ref.dtype)

    @pl.when(i >= nused)
    def _():
        o_ref[...] = jnp.zeros_like(o_ref)


def moe_dispatch(h, w, tok, nused, *, tg=MOE_ROW_TILE):
    d = h.shape[1]
    rows = tok.shape[0]
    return pl.pallas_call(
        _dispatch_kernel,
        grid_spec=pltpu.PrefetchScalarGridSpec(
            num_scalar_prefetch=2,
            grid=(rows // tg,),
            in_specs=[pl.BlockSpec(memory_space=pl.ANY), pl.BlockSpec((1, d), lambda i, t, n: (0, 0))],
            out_specs=pl.BlockSpec((tg, d), lambda i, t, n: (i, 0)),
            scratch_shapes=[pltpu.VMEM((2, tg, d), F32), pltpu.SemaphoreType.DMA((2,))]),
        out_shape=jax.ShapeDtypeStruct((rows, d), BF16),
        compiler_params=_compiler_params(("arbitrary",)),
    )(tok, nused, h, w.reshape(1, d))


def _combine_kernel(pos1_ref, pos2_ref, y_hbm, h_hbm, route_hbm, w_ref, o_ref,
                    y1_ref, y2_ref, h_ref, r_ref, sem_y, sem_h, *, n_pad, n_frames):
    i = pl.program_id(0)
    n_tiles = pl.num_programs(0)
    tr = h_ref.shape[1]
    tiles_per_batch = n_frames // tr
    slot = i % 2

    def block_copies(tile, dst_slot):
        b = tile // tiles_per_batch
        row0 = b * n_pad + N_META + (tile - b * tiles_per_batch) * tr
        return (pltpu.make_async_copy(h_hbm.at[pl.ds(row0, tr)], h_ref.at[dst_slot], sem_h.at[0, dst_slot]),
                pltpu.make_async_copy(route_hbm.at[pl.ds(row0, tr)], r_ref.at[dst_slot], sem_h.at[1, dst_slot]))

    def y_copies(row1, row2, dst_slot, r):
        return (pltpu.make_async_copy(y_hbm.at[pl.ds(row1, 1)], y1_ref.at[dst_slot, pl.ds(r, 1)], sem_y.at[dst_slot]),
                pltpu.make_async_copy(y_hbm.at[pl.ds(row2, 1)], y2_ref.at[dst_slot, pl.ds(r, 1)], sem_y.at[dst_slot]))

    def start_tile(tile, dst_slot):
        for cp in block_copies(tile, dst_slot):
            cp.start()

        def body(r, carry):
            t = tile * tr + r
            for cp in y_copies(pos1_ref[t], pos2_ref[t], dst_slot, r):
                cp.start()
            return carry
        lax.fori_loop(0, tr, body, 0, unroll=8)

    def wait_tile(tile, dst_slot):
        for cp in block_copies(tile, dst_slot):
            cp.wait()

        def body(r, carry):
            for cp in y_copies(0, 0, dst_slot, r):
                cp.wait()
            return carry
        lax.fori_loop(0, tr, body, 0, unroll=8)

    @pl.when(i == 0)
    def _():
        start_tile(0, 0)

    @pl.when(i + 1 < n_tiles)
    def _():
        start_tile(i + 1, 1 - slot)

    wait_tile(i, slot)
    route = r_ref[slot]
    lane = lax.broadcasted_iota(jnp.int32, route.shape, 1).astype(F32)
    in_experts = lane < N_EXPERTS
    w1 = jnp.sum(jnp.where(jnp.logical_and(in_experts, lane == route[:, N_EXPERTS:N_EXPERTS + 1]), route, 0.0),
                 axis=-1, keepdims=True)
    w2 = jnp.sum(jnp.where(jnp.logical_and(in_experts, lane == route[:, N_EXPERTS + 1:N_EXPERTS + 2]), route, 0.0),
                 axis=-1, keepdims=True)
    x = h_ref[slot] + w1 * y1_ref[slot] + w2 * y2_ref[slot]
    ms = jnp.mean(x * x, axis=-1, keepdims=True)
    o_ref[...] = x * lax.rsqrt(ms + NORM_EPS) * w_ref[...]


def moe_combine(ys, h, route, final_norm, pos1, pos2, *, bsz, n_pad, n_frames, tr=MOE_ROW_TILE):
    d = h.shape[1]
    kernel = functools.partial(_combine_kernel, n_pad=n_pad, n_frames=n_frames)
    any_spec = pl.BlockSpec(memory_space=pl.ANY)
    return pl.pallas_call(
        kernel,
        grid_spec=pltpu.PrefetchScalarGridSpec(
            num_scalar_prefetch=2,
            grid=(bsz * n_frames // tr,),
            in_specs=[any_spec, any_spec, any_spec, pl.BlockSpec((1, d), lambda i, p1, p2: (0, 0))],
            out_specs=pl.BlockSpec((tr, d), lambda i, p1, p2: (i, 0)),
            scratch_shapes=[pltpu.VMEM((2, tr, d), F32), pltpu.VMEM((2, tr, d), F32), pltpu.VMEM((2, tr, d), F32),
                            pltpu.VMEM((2, tr, LANES), F32), pltpu.SemaphoreType.DMA((2,)),
                            pltpu.SemaphoreType.DMA((2, 2))]),
        out_shape=jax.ShapeDtypeStruct((bsz * n_frames, d), F32),
        compiler_params=_compiler_params(("arbitrary",)),
    )(pos1, pos2, ys, h, route, final_norm.reshape(1, d))


def _moe_plan(route, *, bsz, n_pad, n_frames, tm):
    n_tok = bsz * n_frames
    cap = 2 * n_tok + N_EXPERTS * tm
    frames = route.reshape(bsz, n_pad, LANES)[:, N_META:N_META + n_frames].reshape(n_tok, LANES)
    idx = frames[:, N_EXPERTS:N_EXPERTS + 2].astype(jnp.int32)
    sel = (idx[:, :, None] == jnp.arange(N_EXPERTS)[None, None, :]).any(axis=1)
    counts = sel.sum(axis=0)
    padded = ((counts + tm - 1) // tm) * tm
    ends = jnp.cumsum(padded)
    starts = ends - padded
    rank = jnp.cumsum(sel, axis=0) - 1
    pos = starts[None, :] + rank
    pos12 = jnp.take_along_axis(pos, idx, axis=1).astype(jnp.int32)
    tok_rows = (jnp.arange(n_tok) // n_frames) * n_pad + N_META + jnp.arange(n_tok) % n_frames
    tok = jnp.zeros((cap,), jnp.int32).at[pos12.reshape(-1)].set(
        jnp.repeat(tok_rows.astype(jnp.int32), 2), mode="drop")
    tile_start = jnp.arange(cap // tm) * tm
    tile_grp = jnp.minimum((tile_start[:, None] >= ends[None, :]).sum(axis=1), N_EXPERTS - 1).astype(jnp.int32)
    nused = (ends[-1] // tm).astype(jnp.int32).reshape(1)
    return tok, tile_grp, nused, pos12[:, 0], pos12[:, 1]


def _dense_plan(m, tm, group):
    return jnp.full((m // tm,), group, jnp.int32), jnp.full((1,), m // tm, jnp.int32)


def mixer_tables(n_pad):
    diag, col = attention_tables(n_pad)
    expand = (jnp.arange(LANES)[:, None] == (jnp.arange(SSM_WIDTH) // SSM_HEAD_DIM)[None, :]).astype(BF16)
    return {"attn_diag": diag, "attn_col": col, "expand": expand}


def mixer_layer(h, layer, p, *, bsz, n_pad, tm):
    m, d = h.shape
    lambda_init = 0.8 - 0.6 * math.exp(-0.3 * layer)
    grp, nused = _dense_plan(m, tm, layer)
    mm = functools.partial(grouped_matmul, grp=grp, nused=nused, tm=tm)
    hn = rmsnorm(h, p["mix_norm"][layer])
    w_in = p["mix_w_in"]
    qkv = mm([hn], [w_in], [0], [], _epi_plain, ncols=QKV_COLS, tn=512, out_dtype=BF16)
    zx = mm([hn], [w_in], [0], [], _epi_plain, ncols=ZX_COLS, tn=512, out_dtype=F32, w_col0=(ZX_COL0,))
    dtr = mm([hn], [w_in], [0], [], _epi_plain, ncols=LANES, tn=LANES, out_dtype=F32, w_col0=(DT_COL0,))
    gates = mm([hn], [w_in], [0], [], _epi_sigmoid, ncols=2 * d, tn=512, out_dtype=BF16, w_col0=(DT_COL0,),
               w_shift=(GATE_COL0 - DT_COL0,))
    lam_params = jnp.stack([p["lambda_q1"][layer], p["lambda_k1"][layer], p["lambda_q2"][layer],
                            p["lambda_k2"][layer]]).astype(F32)
    o_attn = diff_attention(qkv, lam_params, p["diff_norm"][layer], p["attn_diag"], p["attn_col"], bsz=bsz,
                            n_pad=n_pad, lambda_init=lambda_init)
    o_ssm = ssd_mixer(zx, dtr, p["conv_w"][layer], p["conv_b"][layer], p["dt_bias"][layer], p["a_log"][layer],
                      p["d_skip"][layer], p["ssm_norm"][layer], p["expand"], bsz=bsz, n_pad=n_pad)
    merged = mm([o_attn, o_ssm], [p["w_proj_attn"], p["w_proj_ssm"]], [0, 1], [gates, gates], _epi_gated_sum,
                ncols=d, tn=512, out_dtype=BF16, extra_col0=(0, d))
    return mm([merged], [p["w_out"]], [0], [h], _epi_residual, ncols=d, tn=512, out_dtype=F32, extra_col0=(0,))


def dense_ffn(h, norm_w, w_gate, w_up, w_down, group, *, tm):
    m, d = h.shape
    d_ff = w_gate.shape[2]
    grp, nused = _dense_plan(m, tm, group)
    mm = functools.partial(grouped_matmul, grp=grp, nused=nused, tm=tm)
    hn = rmsnorm(h, norm_w)
    u = mm([hn], [w_gate, w_up], [0, 0], [], _epi_swiglu, ncols=d_ff, tn=256, out_dtype=BF16)
    tm_down = tm // 2
    grp, nused = _dense_plan(m, tm_down, group)
    return grouped_matmul([u], [w_down.astype(BF16)], [0], [h], _epi_residual, grp=grp, nused=nused, ncols=d,
                          tm=tm_down, tn=512, out_dtype=F32, extra_col0=(0,))


def moe_ffn_final(h, norm_w, router_w, w_gate, w_up, w_down, final_norm, group, *, bsz, n_pad, n_frames):
    d = h.shape[1]
    d_fe = w_gate.shape[3]
    r_pad = jnp.pad(router_w.astype(F32), ((0, 0), (0, LANES - N_EXPERTS)))
    route = route_tokens(h, norm_w, r_pad)
    tok, tile_grp, tiles_used, pos1, pos2 = _moe_plan(route, bsz=bsz, n_pad=n_pad, n_frames=n_frames,
                                                       tm=MOE_ROW_TILE)
    xs = moe_dispatch(h, norm_w, tok, tiles_used)
    mm = functools.partial(grouped_matmul, grp=tile_grp + group * N_EXPERTS, nused=tiles_used, tm=MOE_ROW_TILE)
    u = mm([xs], [w_gate.reshape(-1, d, d_fe), w_up.reshape(-1, d, d_fe)], [0, 0], [], _epi_swiglu,
           ncols=d_fe, tn=512, out_dtype=BF16)
    ys = mm([u], [w_down.reshape(-1, d_fe, d)], [0], [], _epi_plain, ncols=d, tn=1024, out_dtype=F32)
    return moe_combine(ys, h, route, final_norm, pos1, pos2, bsz=bsz, n_pad=n_pad, n_frames=n_frames)


def kernel(x, meta_tokens, mix_norm, mix_w_in, conv_w, conv_b, dt_bias, a_log, d_skip, ssm_norm, lambda_q1, lambda_k1, lambda_q2, lambda_k2, diff_norm, w_proj_attn, w_proj_ssm, w_out, ffn_norm, ffn_w_gate, ffn_w_up, ffn_w_down, router, moe_w_gate, moe_w_up, moe_w_down, final_norm):
    bsz, n_frames, d = x.shape
    assert mix_norm.shape[0] == 2 and d == D_MODEL, "layer 0: dense FFN; layer 1: MoE FFN, then the final norm"
    n = N_META + n_frames
    n_pad = -(-n // Q_BLOCK) * Q_BLOCK
    m = bsz * n_pad
    tm = n_pad // 2
    assert tm % 16 == 0 and n_pad % SSD_BLOCK == 0 and n_frames % MOE_ROW_TILE == 0

    meta = jnp.broadcast_to(meta_tokens[None].astype(x.dtype), (bsz, N_META, d))
    h = jnp.concatenate([meta, x, jnp.zeros((bsz, n_pad - n, d), x.dtype)], axis=1).reshape(m, d)

    p = dict(mix_norm=mix_norm, mix_w_in=mix_w_in, conv_w=conv_w, conv_b=conv_b, dt_bias=dt_bias, a_log=a_log,
             d_skip=d_skip, ssm_norm=ssm_norm, lambda_q1=lambda_q1, lambda_k1=lambda_k1, lambda_q2=lambda_q2,
             lambda_k2=lambda_k2, diff_norm=diff_norm, w_proj_attn=w_proj_attn, w_proj_ssm=w_proj_ssm, w_out=w_out)
    p.update(mixer_tables(n_pad))
    h = mixer_layer(h, 0, p, bsz=bsz, n_pad=n_pad, tm=tm)
    h = dense_ffn(h, ffn_norm[0], ffn_w_gate, ffn_w_up, ffn_w_down, 0, tm=tm)
    h = mixer_layer(h, 1, p, bsz=bsz, n_pad=n_pad, tm=tm)
    out = moe_ffn_final(h, ffn_norm[1], router[0], moe_w_gate, moe_w_up, moe_w_down, final_norm, 0,
                        bsz=bsz, n_pad=n_pad, n_frames=n_frames)
    return out.reshape(bsz, n_frames, d)
```

```python
import functools
import math

import jax
import jax.numpy as jnp
from jax import lax
from jax.experimental import pallas as pl
from jax.experimental.pallas import tpu as pltpu

F32 = jnp.float32
BF16 = jnp.bfloat16

LANES = 128
VMEM_LIMIT_BYTES_V7X = 56 * 2**20

D_MODEL = 4096
N_META = 16
CHUNK = 64
Q_BLOCK = 128
HEADS = 8
QK_DIM = 128
V_DIM = 256
ATTN_WIDTH = HEADS * V_DIM
SUBLN_EPS = 1e-5
SSM_WIDTH = 2048
SSM_HEAD_DIM = 64
SSM_HEADS = 32
SSM_GROUPS = 4
SSM_STATE = 128
SSM_CONV = 4
SSM_GROUP_WIDTH = SSM_WIDTH // SSM_GROUPS
XBC_WIDTH = SSM_WIDTH + 2 * SSM_GROUPS * SSM_STATE
QKV_COLS = 3 * ATTN_WIDTH
ZX_COL0 = QKV_COLS
ZX_COLS = SSM_WIDTH + XBC_WIDTH
DT_COL0 = ZX_COL0 + ZX_COLS
GATE_COL0 = DT_COL0 + SSM_HEADS
N_EXPERTS = 8
NORM_EPS = 1e-6
NEG_BIG = -1e30
SSD_BLOCK = 128

ROW_TILE = 1088
MOE_ROW_TILE = 256
MOE_MM_ROW_TILE = 512
NORM_ROW_TILE = 272


def _silu(x):
    return x * (1.0 / (1.0 + jnp.exp(-x)))


def _sigmoid(x):
    return 1.0 / (1.0 + jnp.exp(-x))


def _compiler_params(semantics):
    return pltpu.CompilerParams(dimension_semantics=semantics, vmem_limit_bytes=VMEM_LIMIT_BYTES_V7X)


def _rmsnorm_kernel(x_ref, w_ref, o_ref, *, eps):
    x = x_ref[...]
    ms = jnp.mean(x * x, axis=-1, keepdims=True)
    o_ref[...] = (x * lax.rsqrt(ms + eps) * w_ref[...]).astype(o_ref.dtype)


def _norm_row_tile(m):
    return NORM_ROW_TILE if m % NORM_ROW_TILE == 0 else Q_BLOCK


def rmsnorm(x, w, *, eps=NORM_EPS, out_dtype=BF16):
    m, d = x.shape
    tr = _norm_row_tile(m)
    return pl.pallas_call(
        functools.partial(_rmsnorm_kernel, eps=eps),
        grid=(m // tr,),
        in_specs=[pl.BlockSpec((tr, d), lambda i: (i, 0)), pl.BlockSpec((1, d), lambda i: (0, 0))],
        out_specs=pl.BlockSpec((tr, d), lambda i: (i, 0)),
        out_shape=jax.ShapeDtypeStruct((m, d), out_dtype),
        compiler_params=_compiler_params(("parallel",)),
    )(x, w.reshape(1, d))


def _split_bf16(x):
    hi = x.astype(BF16)
    lo = (x - hi.astype(F32)).astype(BF16)
    return hi, lo


def _router_kernel(x_ref, w_ref, r_ref, o_ref, *, eps):
    x = x_ref[...]
    ms = jnp.mean(x * x, axis=-1, keepdims=True)
    hn = x * lax.rsqrt(ms + eps) * w_ref[...]
    h_hi, h_lo = _split_bf16(hn)
    r_hi, r_lo = _split_bf16(r_ref[...])
    logits = (jnp.dot(h_hi, r_hi, preferred_element_type=F32)
              + jnp.dot(h_hi, r_lo, preferred_element_type=F32)
              + jnp.dot(h_lo, r_hi, preferred_element_type=F32))
    lane = lax.broadcasted_iota(jnp.int32, logits.shape, 1).astype(F32)
    lg = jnp.where(lane < N_EXPERTS, logits, -jnp.inf)
    m1 = jnp.max(lg, axis=-1, keepdims=True)
    i1 = jnp.min(jnp.where(lg == m1, lane, float(LANES)), axis=-1, keepdims=True)
    lg2 = jnp.where(lane == i1, -jnp.inf, lg)
    m2 = jnp.max(lg2, axis=-1, keepdims=True)
    i2 = jnp.min(jnp.where(lg2 == m2, lane, float(LANES)), axis=-1, keepdims=True)
    e2 = jnp.exp(m2 - m1)
    den = 1.0 + e2
    out = jnp.where(lane == i1, 1.0 / den, 0.0) + jnp.where(lane == i2, e2 / den, 0.0)
    out = jnp.where(lane == N_EXPERTS, i1, out)
    out = jnp.where(lane == N_EXPERTS + 1, i2, out)
    o_ref[...] = out


def route_tokens(x, w, r_pad):
    m, d = x.shape
    tr = _norm_row_tile(m)
    return pl.pallas_call(
        functools.partial(_router_kernel, eps=NORM_EPS),
        grid=(m // tr,),
        in_specs=[pl.BlockSpec((tr, d), lambda i: (i, 0)), pl.BlockSpec((1, d), lambda i: (0, 0)),
                  pl.BlockSpec((d, LANES), lambda i: (0, 0))],
        out_specs=pl.BlockSpec((tr, LANES), lambda i: (i, 0)),
        out_shape=jax.ShapeDtypeStruct((m, LANES), F32),
        compiler_params=_compiler_params(("parallel",)),
    )(x, w.reshape(1, d), r_pad)


def _mm_kernel(grp_ref, nused_ref, *refs, n_a, w_to_a, cast, w_shift, w_transposed, n_extra, nk, epilogue):
    col_axis = 0 if w_transposed else 1
    n_w = len(w_to_a)
    a_refs = refs[:n_a]
    w_refs = refs[n_a:n_a + n_w]
    refs = list(refs[n_a + n_w:])
    wnext_refs = [refs.pop(0) if s else None for s in w_shift]
    e_refs = [refs.pop(0) for _ in range(n_extra)]
    o_ref = refs.pop(0)
    scratch = refs
    wb_refs = [scratch.pop(0) if c else None for c in cast]
    acc_refs = [scratch.pop(0) for _ in range(n_w)] if nk > 1 else []
    i = pl.program_id(1)
    k = pl.program_id(2)

    @pl.when(i >= nused_ref[0])
    def _():
        o_ref[...] = jnp.zeros_like(o_ref)

    @pl.when(i < nused_ref[0])
    def _():
        if any(cast):
            def do_cast():
                for w_ref, wn_ref, wb_ref, s in zip(w_refs, wnext_refs, wb_refs, w_shift):
                    if wb_ref is None:
                        continue
                    if s:
                        tn = w_ref.shape[col_axis]
                        wide = jnp.concatenate([w_ref[...], wn_ref[...]], axis=col_axis)
                        wide = wide[s:s + tn, :] if w_transposed else wide[:, s:s + tn]
                        wb_ref[...] = wide.astype(BF16)
                    else:
                        wb_ref[...] = w_ref[...].astype(BF16)
            if nk == 1:
                changed = jnp.logical_or(i == 0, grp_ref[i] != grp_ref[jnp.maximum(i - 1, 0)])
                pl.when(changed)(do_cast)
            else:
                do_cast()
        parts = []
        for wi in range(n_w):
            w = wb_refs[wi][...] if cast[wi] else w_refs[wi][...]
            contract = (((1,), (1,)), ((), ())) if w_transposed else (((1,), (0,)), ((), ()))
            parts.append(lax.dot_general(a_refs[w_to_a[wi]][...], w, contract, preferred_element_type=F32))
        if nk == 1:
            o_ref[...] = epilogue(parts, e_refs).astype(o_ref.dtype)
        else:
            @pl.when(k == 0)
            def _():
                for acc, p in zip(acc_refs, parts):
                    acc[...] = p

            @pl.when(k > 0)
            def _():
                for acc, p in zip(acc_refs, parts):
                    acc[...] += p

            @pl.when(k == nk - 1)
            def _():
                o_ref[...] = epilogue([acc[...] for acc in acc_refs], e_refs).astype(o_ref.dtype)


def grouped_matmul(a_list, w_list, w_to_a, extras, epilogue, *, grp, nused, ncols, tm, tn, out_dtype,
                   w_col0=None, w_shift=None, w_transposed=False, extra_col0=(), tk=None):
    m, kdim = a_list[0].shape
    tk = kdim if tk is None else tk
    nk = kdim // tk
    w_col0 = (0,) * len(w_list) if w_col0 is None else tuple(w_col0)
    w_shift = (0,) * len(w_list) if w_shift is None else tuple(w_shift)
    assert m % tm == 0 and ncols % tn == 0 and kdim % tk == 0 and all(c % tn == 0 for c in w_col0)
    assert all(a.shape == (m, kdim) for a in a_list)
    assert all(w.shape[2 if w_transposed else 1] == kdim for w in w_list)
    cast = tuple(w.dtype != BF16 for w in w_list)
    assert all(c or not s for c, s in zip(cast, w_shift)) and all(0 <= s < LANES for s in w_shift)
    assert not w_transposed or all(s % 8 == 0 for s in w_shift)
    ecb = tuple(c // tn for c in extra_col0)
    lane_tiles = tn // LANES

    def row(i, n):
        return jnp.minimum(i, n[0] - 1)

    def w_block(width, col_index):
        if w_transposed:
            return pl.BlockSpec((None, width, tk), lambda j, i, k, g, n: (g[row(i, n)], col_index(j), k))
        return pl.BlockSpec((None, tk, width), lambda j, i, k, g, n: (g[row(i, n)], k, col_index(j)))

    def w_spec(cb):
        return w_block(tn, lambda j: cb + j)

    def w_next_spec(cb):
        return w_block(LANES, lambda j: (cb + j + 1) * lane_tiles)

    in_specs = [pl.BlockSpec((tm, tk), lambda j, i, k, g, n: (row(i, n), k)) for _ in a_list]
    in_specs += [w_spec(c // tn) for c in w_col0]
    in_specs += [w_next_spec(c // tn) for c, s in zip(w_col0, w_shift) if s]
    in_specs += [pl.BlockSpec((tm, tn), functools.partial(lambda j, i, k, g, n, eb: (row(i, n), eb + j), eb=eb))
                 for eb in ecb]
    scratch = [pltpu.VMEM((tn, tk) if w_transposed else (tk, tn), BF16) for c in cast if c]
    if nk > 1:
        scratch += [pltpu.VMEM((tm, tn), F32) for _ in w_list]
    kernel = functools.partial(_mm_kernel, n_a=len(a_list), w_to_a=tuple(w_to_a), cast=cast, w_shift=w_shift,
                               w_transposed=w_transposed, n_extra=len(extras), nk=nk, epilogue=epilogue)
    w_next = [w for w, s in zip(w_list, w_shift) if s]
    return pl.pallas_call(
        kernel,
        grid_spec=pltpu.PrefetchScalarGridSpec(
            num_scalar_prefetch=2,
            grid=(ncols // tn, m // tm, nk),
            in_specs=in_specs,
            out_specs=pl.BlockSpec((tm, tn), lambda j, i, k, g, n: (i, j)),
            scratch_shapes=scratch),
        out_shape=jax.ShapeDtypeStruct((m, ncols), out_dtype),
        compiler_params=_compiler_params(("arbitrary", "arbitrary", "arbitrary")),
    )(grp, nused, *a_list, *w_list, *w_next, *extras)


def _epi_plain(parts, e_refs):
    return parts[0]


def _epi_sigmoid(parts, e_refs):
    return _sigmoid(parts[0])


def _epi_gated_sum(parts, e_refs):
    return e_refs[0][...] * parts[0] + e_refs[1][...] * parts[1]


def _epi_residual(parts, e_refs):
    return e_refs[0][...] + parts[0]


def _epi_swiglu(parts, e_refs):
    return _silu(parts[0]) * parts[1]


def attention_tables(n_pad):
    slopes = jnp.exp2(-8.0 * (jnp.arange(HEADS, dtype=F32) + 1.0) / HEADS)
    i = jnp.arange(Q_BLOCK)[:, None]
    j = jnp.arange(2 * Q_BLOCK)[None, :]
    visible = (i - N_META) // CHUNK >= (j - N_META) // CHUNK
    rel = (i - jnp.abs(i - j)).astype(F32)
    diag = jnp.where(visible[None], slopes[:, None, None] * rel[None], NEG_BIG)
    col = slopes[:, None, None] * jnp.arange(n_pad, dtype=F32)[None, None, :]
    return diag, col


def _attn_kernel(q_ref, k_ref, v_ref, lam_ref, dn_ref, diag_ref, col_ref, o_ref, kt_ref, *, lambda_init):
    n_pad = q_ref.shape[0]
    lp = lam_ref[...]
    s1 = jnp.sum(lp[0:1] * lp[1:2], axis=-1, keepdims=True)
    s2 = jnp.sum(lp[2:3] * lp[3:4], axis=-1, keepdims=True)
    lam = jnp.exp(s1) - jnp.exp(s2) + lambda_init
    for mp in range(2):
        kt_ref[mp] = k_ref[:, mp * QK_DIM:(mp + 1) * QK_DIM].T
    scale = QK_DIM ** -0.5
    gain = dn_ref[...] * (1.0 - lambda_init)
    for qb in range(n_pad // Q_BLOCK):
        q0 = qb * Q_BLOCK
        width = min(2 * Q_BLOCK, n_pad - q0)
        rows = slice(q0, q0 + Q_BLOCK)
        diag_bias = diag_ref[:, :width]
        past_bias = col_ref[:, :q0] - col_ref[:, q0:q0 + 1] if qb else None
        a_diag = None
        a_past = None
        for mp in range(2):
            q = q_ref[rows, mp * QK_DIM:(mp + 1) * QK_DIM]
            s_d = jnp.dot(q, kt_ref[mp, :, q0:q0 + width], preferred_element_type=F32) * scale + diag_bias
            mx = jnp.max(s_d, axis=-1, keepdims=True)
            if qb:
                s_p = jnp.dot(q, kt_ref[mp, :, :q0], preferred_element_type=F32) * scale + past_bias
                mx = jnp.maximum(mx, jnp.max(s_p, axis=-1, keepdims=True))
                e_p = jnp.exp(s_p - mx)
            e_d = jnp.exp(s_d - mx)
            den = jnp.sum(e_d, axis=-1, keepdims=True)
            if qb:
                den = den + jnp.sum(e_p, axis=-1, keepdims=True)
            coef = 1.0 / den if mp == 0 else -lam / den
            a_diag = e_d * coef if mp == 0 else a_diag + e_d * coef
            if qb:
                a_past = e_p * coef if mp == 0 else a_past + e_p * coef
        o = jnp.dot(a_diag.astype(BF16), v_ref[q0:q0 + width, :], preferred_element_type=F32)
        if qb:
            o = o + jnp.dot(a_past.astype(BF16), v_ref[:q0, :], preferred_element_type=F32)
        ms = jnp.mean(o * o, axis=-1, keepdims=True)
        o_ref[rows, :] = (o * lax.rsqrt(ms + SUBLN_EPS) * gain).astype(o_ref.dtype)


def diff_attention(qkv, lam_params, diff_norm, diag, col, *, bsz, n_pad, lambda_init):
    m = qkv.shape[0]
    kernel = functools.partial(_attn_kernel, lambda_init=lambda_init)
    return pl.pallas_call(
        kernel,
        grid=(bsz, HEADS),
        in_specs=[
            pl.BlockSpec((n_pad, V_DIM), lambda b, h: (b, h)),
            pl.BlockSpec((n_pad, V_DIM), lambda b, h: (b, HEADS + h)),
            pl.BlockSpec((n_pad, V_DIM), lambda b, h: (b, 2 * HEADS + h)),
            pl.BlockSpec((4, QK_DIM), lambda b, h: (0, 0)),
            pl.BlockSpec((1, V_DIM), lambda b, h: (0, 0)),
            pl.BlockSpec((None, Q_BLOCK, 2 * Q_BLOCK), lambda b, h: (h, 0, 0)),
            pl.BlockSpec((None, 1, n_pad), lambda b, h: (h, 0, 0)),
        ],
        out_specs=pl.BlockSpec((n_pad, V_DIM), lambda b, h: (b, h)),
        out_shape=jax.ShapeDtypeStruct((m, ATTN_WIDTH), BF16),
        scratch_shapes=[pltpu.VMEM((2, QK_DIM, n_pad), BF16)],
        compiler_params=_compiler_params(("parallel", "parallel")),
    )(qkv, qkv, qkv, lam_params, diff_norm.reshape(1, V_DIM), diag, col)


def _split3_bf16(x):
    hi = x.astype(BF16)
    r1 = x - hi.astype(F32)
    mid = r1.astype(BF16)
    lo = (r1 - mid.astype(F32)).astype(BF16)
    return hi, mid, lo


def _ssd_kernel(zx_ref, dtr_ref, cw_ref, cb_ref, dtb_ref, alog_ref, dsk_ref, nw_ref, ex_ref, o_ref,
                state_ref, xbuf_ref, y_ref):
    c = pl.program_id(1)
    L = SSD_BLOCK
    halo = 8

    @pl.when(c == 0)
    def _():
        state_ref[...] = jnp.zeros_like(state_ref)
        xbuf_ref[0:halo, :] = jnp.zeros((halo, XBC_WIDTH), F32)

    @pl.when(c > 0)
    def _():
        xbuf_ref[0:halo, :] = xbuf_ref[L:L + halo, :]

    u = zx_ref[:, SSM_WIDTH:SSM_WIDTH + XBC_WIDTH]
    xbuf_ref[halo:halo + L, :] = u
    acc = cb_ref[...] + cw_ref[SSM_CONV - 1:SSM_CONV, :] * u
    for s in range(1, SSM_CONV):
        acc = acc + cw_ref[SSM_CONV - 1 - s:SSM_CONV - s, :] * xbuf_ref[halo - s:halo - s + L, :]
    xbc = _silu(acc)
    xs = xbc[:, :SSM_WIDTH]
    bm = xbc[:, SSM_WIDTH:SSM_WIDTH + SSM_GROUPS * SSM_STATE].astype(BF16)
    cm = xbc[:, SSM_WIDTH + SSM_GROUPS * SSM_STATE:].astype(BF16)

    pre = dtr_ref[...] + dtb_ref[...]
    dt = jnp.maximum(pre, 0.0) + jnp.log(1.0 + jnp.exp(-jnp.abs(pre)))
    adt = dt * (-jnp.exp(alog_ref[...]))
    rows = lax.broadcasted_iota(jnp.int32, (L, LANES), 0)
    cum = adt
    shift = 1
    while shift < L:
        cum = cum + jnp.where(rows >= shift, pltpu.roll(cum, shift, 0), 0.0)
        shift *= 2
    cum_last = cum[L - 1:L, :]
    ecum = jnp.exp(cum)
    edec = jnp.exp(cum_last - cum)
    elast = jnp.broadcast_to(jnp.exp(cum_last), (8, LANES))
    stacked = jnp.concatenate([dt, ecum, edec, elast], axis=0)
    ex = ex_ref[...]
    wide = sum(jnp.dot(p, ex, preferred_element_type=F32) for p in _split3_bf16(stacked))
    dt_w, ecum_w, edec_w, elast_w = wide[0:L], wide[L:2 * L], wide[2 * L:3 * L], wide[3 * L:3 * L + 1]

    xd = xs * dt_w
    xd_b = xd.astype(BF16)
    xdec_b = (xd * edec_w).astype(BF16)
    cum_t = cum.T
    tril = lax.broadcasted_iota(jnp.int32, (L, L), 0) >= lax.broadcasted_iota(jnp.int32, (L, L), 1)
    heads_per_group = SSM_HEADS // SSM_GROUPS
    for g in range(SSM_GROUPS):
        gs = slice(g * SSM_GROUP_WIDTH, (g + 1) * SSM_GROUP_WIDTH)
        bg = bm[:, g * SSM_STATE:(g + 1) * SSM_STATE]
        cg = cm[:, g * SSM_STATE:(g + 1) * SSM_STATE]
        cb = lax.dot_general(cg, bg, (((1,), (1,)), ((), ())), preferred_element_type=F32)
        st = state_ref[:, gs]
        y_off = jnp.dot(cg, st.astype(BF16), preferred_element_type=F32) * ecum_w[:, gs]
        state_ref[:, gs] = elast_w[:, gs] * st + lax.dot_general(
            bg, xdec_b[:, gs], (((0,), (0,)), ((), ())), preferred_element_type=F32)
        y_ref[:, gs] = y_off
        for r in range(heads_per_group):
            hd = g * heads_per_group + r
            hs = slice(hd * SSM_HEAD_DIM, (hd + 1) * SSM_HEAD_DIM)
            seg = cum[:, hd:hd + 1] - cum_t[hd:hd + 1, :]
            lmat = jnp.exp(jnp.where(tril, seg, -jnp.inf))
            y_ref[:, hs] += jnp.dot((cb * lmat).astype(BF16), xd_b[:, hs], preferred_element_type=F32)

    y = y_ref[...] + dsk_ref[...] * xs
    gated = y * _silu(zx_ref[:, 0:SSM_WIDTH])
    for g in range(SSM_GROUPS):
        gs = slice(g * SSM_GROUP_WIDTH, (g + 1) * SSM_GROUP_WIDTH)
        gg = gated[:, gs]
        ms = jnp.mean(gg * gg, axis=-1, keepdims=True)
        o_ref[:, gs] = (gg * lax.rsqrt(ms + NORM_EPS) * nw_ref[:, gs]).astype(o_ref.dtype)


def ssd_mixer(zx, dtr, conv_w, conv_b, dt_bias, a_log, d_skip, ssm_norm, expand, *, bsz, n_pad):
    m = zx.shape[0]
    nc = n_pad // SSD_BLOCK
    L = SSD_BLOCK

    def pad_heads(v):
        return jnp.pad(v.astype(F32), (0, LANES - SSM_HEADS)).reshape(1, LANES)

    def const(shape):
        return pl.BlockSpec(shape, lambda b, c: (0, 0))

    return pl.pallas_call(
        _ssd_kernel,
        grid=(bsz, nc),
        in_specs=[
            pl.BlockSpec((L, ZX_COLS), lambda b, c: (b * nc + c, 0)),
            pl.BlockSpec((L, LANES), lambda b, c: (b * nc + c, 0)),
            const((SSM_CONV, XBC_WIDTH)), const((1, XBC_WIDTH)), const((1, LANES)), const((1, LANES)),
            const((1, SSM_WIDTH)), const((1, SSM_WIDTH)), const((LANES, SSM_WIDTH)),
        ],
        out_specs=pl.BlockSpec((L, SSM_WIDTH), lambda b, c: (b * nc + c, 0)),
        out_shape=jax.ShapeDtypeStruct((m, SSM_WIDTH), BF16),
        scratch_shapes=[pltpu.VMEM((SSM_STATE, SSM_WIDTH), F32), pltpu.VMEM((L + 8, XBC_WIDTH), F32),
                        pltpu.VMEM((L, SSM_WIDTH), F32)],
        compiler_params=_compiler_params(("parallel", "arbitrary")),
    )(zx, dtr, conv_w, conv_b.reshape(1, XBC_WIDTH), pad_heads(dt_bias), pad_heads(a_log),
      jnp.repeat(d_skip.astype(F32), SSM_HEAD_DIM).reshape(1, SSM_WIDTH), ssm_norm.reshape(1, SSM_WIDTH), expand)


def _dispatch_kernel(tok_ref, nused_ref, h_hbm, w_ref, o_ref, buf_ref, sem):
    i = pl.program_id(0)
    g = buf_ref.shape[1]
    nused = nused_ref[0]
    slot = i % 2

    def row_copy(src_row, dst_slot, r):
        return pltpu.make_async_copy(h_hbm.at[pl.ds(src_row, 1)], buf_ref.at[dst_slot, pl.ds(r, 1)],
                                     sem.at[dst_slot])

    def start_tile(tile, dst_slot):
        def body(r, carry):
            row_copy(tok_ref[tile * g + r], dst_slot, r).start()
            return carry
        lax.fori_loop(0, g, body, 0, unroll=8)

    def wait_tile(dst_slot):
        def body(r, carry):
            row_copy(0, dst_slot, r).wait()
            return carry
        lax.fori_loop(0, g, body, 0, unroll=8)

    @pl.when(jnp.logical_and(i == 0, nused > 0))
    def _():
        start_tile(0, 0)

    @pl.when(i + 1 < nused)
    def _():
        start_tile(i + 1, 1 - slot)

    @pl.when(i < nused)
    def _():
        wait_tile(slot)
        x = buf_ref[slot]
        ms = jnp.mean(x * x, axis=-1, keepdims=True)
        o_ref[...] = (x * lax.rsqrt(ms + NORM_EPS) * w_ref[...]).astype(o_ref.dtype)

    @pl.when(i >= nused)
    def _():
        o_ref[...] = jnp.zeros_like(o_ref)


def moe_dispatch(h, w, tok, nused, *, tg=MOE_ROW_TILE):
    d = h.shape[1]
    rows = tok.shape[0]
    return pl.pallas_call(
        _dispatch_kernel,
        grid_spec=pltpu.PrefetchScalarGridSpec(
            num_scalar_prefetch=2,
            grid=(rows // tg,),
            in_specs=[pl.BlockSpec(memory_space=pl.ANY), pl.BlockSpec((1, d), lambda i, t, n: (0, 0))],
            out_specs=pl.BlockSpec((tg, d), lambda i, t, n: (i, 0)),
            scratch_shapes=[pltpu.VMEM((2, tg, d), F32), pltpu.SemaphoreType.DMA((2,))]),
        out_shape=jax.ShapeDtypeStruct((rows, d), BF16),
        compiler_params=_compiler_params(("arbitrary",)),
    )(tok, nused, h, w.reshape(1, d))


def _combine_kernel(pos1_ref, pos2_ref, y_hbm, h_hbm, route_hbm, w_ref, o_ref,
                    y1_ref, y2_ref, h_ref, r_ref, sem_y, sem_h, *, n_pad, n_frames):
    i = pl.program_id(0)
    n_tiles = pl.num_programs(0)
    tr = h_ref.shape[1]
    tiles_per_batch = n_frames // tr
    slot = i % 2

    def block_copies(tile, dst_slot):
        b = tile // tiles_per_batch
        row0 = b * n_pad + N_META + (tile - b * tiles_per_batch) * tr
        return (pltpu.make_async_copy(h_hbm.at[pl.ds(row0, tr)], h_ref.at[dst_slot], sem_h.at[0, dst_slot]),
                pltpu.make_async_copy(route_hbm.at[pl.ds(row0, tr)], r_ref.at[dst_slot], sem_h.at[1, dst_slot]))

    def y_copies(row1, row2, dst_slot, r):
        return (pltpu.make_async_copy(y_hbm.at[pl.ds(row1, 1)], y1_ref.at[dst_slot, pl.ds(r, 1)], sem_y.at[dst_slot]),
                pltpu.make_async_copy(y_hbm.at[pl.ds(row2, 1)], y2_ref.at[dst_slot, pl.ds(r, 1)], sem_y.at[dst_slot]))

    def start_tile(tile, dst_slot):
        for cp in block_copies(tile, dst_slot):
            cp.start()

        def body(r, carry):
            t = tile * tr + r
            for cp in y_copies(pos1_ref[t], pos2_ref[t], dst_slot, r):
                cp.start()
            return carry
        lax.fori_loop(0, tr, body, 0, unroll=8)

    def wait_tile(tile, dst_slot):
        for cp in block_copies(tile, dst_slot):
            cp.wait()

        def body(r, carry):
            for cp in y_copies(0, 0, dst_slot, r):
                cp.wait()
            return carry
        lax.fori_loop(0, tr, body, 0, unroll=8)

    @pl.when(i == 0)
    def _():
        start_tile(0, 0)

    @pl.when(i + 1 < n_tiles)
    def _():
        start_tile(i + 1, 1 - slot)

    wait_tile(i, slot)
    route = r_ref[slot]
    lane = lax.broadcasted_iota(jnp.int32, route.shape, 1).astype(F32)
    in_experts = lane < N_EXPERTS
    w1 = jnp.sum(jnp.where(jnp.logical_and(in_experts, lane == route[:, N_EXPERTS:N_EXPERTS + 1]), route, 0.0),
                 axis=-1, keepdims=True)
    w2 = jnp.sum(jnp.where(jnp.logical_and(in_experts, lane == route[:, N_EXPERTS + 1:N_EXPERTS + 2]), route, 0.0),
                 axis=-1, keepdims=True)
    x = h_ref[slot] + w1 * y1_ref[slot] + w2 * y2_ref[slot]
    ms = jnp.mean(x * x, axis=-1, keepdims=True)
    o_ref[...] = x * lax.rsqrt(ms + NORM_EPS) * w_ref[...]


def moe_combine(ys, h, route, final_norm, pos1, pos2, *, bsz, n_pad, n_frames, tr=MOE_ROW_TILE):
    d = h.shape[1]
    kernel = functools.partial(_combine_kernel, n_pad=n_pad, n_frames=n_frames)
    any_spec = pl.BlockSpec(memory_space=pl.ANY)
    return pl.pallas_call(
        kernel,
        grid_spec=pltpu.PrefetchScalarGridSpec(
            num_scalar_prefetch=2,
            grid=(bsz * n_frames // tr,),
            in_specs=[any_spec, any_spec, any_spec, pl.BlockSpec((1, d), lambda i, p1, p2: (0, 0))],
            out_specs=pl.BlockSpec((tr, d), lambda i, p1, p2: (i, 0)),
            scratch_shapes=[pltpu.VMEM((2, tr, d), F32), pltpu.VMEM((2, tr, d), F32), pltpu.VMEM((2, tr, d), F32),
                            pltpu.VMEM((2, tr, LANES), F32), pltpu.SemaphoreType.DMA((2,)),
                            pltpu.SemaphoreType.DMA((2, 2))]),
        out_shape=jax.ShapeDtypeStruct((bsz * n_frames, d), F32),
        compiler_params=_compiler_params(("arbitrary",)),
    )(pos1, pos2, ys, h, route, final_norm.reshape(1, d))


def _moe_plan(route, *, bsz, n_pad, n_frames, tm):
    n_tok = bsz * n_frames
    cap = 2 * n_tok + N_EXPERTS * tm
    frames = route.reshape(bsz, n_pad, LANES)[:, N_META:N_META + n_frames].reshape(n_tok, LANES)
    idx = frames[:, N_EXPERTS:N_EXPERTS + 2].astype(jnp.int32)
    sel = (idx[:, :, None] == jnp.arange(N_EXPERTS)[None, None, :]).any(axis=1)
    counts = sel.sum(axis=0)
    padded = ((counts + tm - 1) // tm) * tm
    ends = jnp.cumsum(padded)
    starts = ends - padded
    rank = jnp.cumsum(sel, axis=0) - 1
    pos = starts[None, :] + rank
    pos12 = jnp.take_along_axis(pos, idx, axis=1).astype(jnp.int32)
    tok_rows = (jnp.arange(n_tok) // n_frames) * n_pad + N_META + jnp.arange(n_tok) % n_frames
    tok = jnp.zeros((cap,), jnp.int32).at[pos12.reshape(-1)].set(
        jnp.repeat(tok_rows.astype(jnp.int32), 2), mode="drop")
    tile_start = jnp.arange(cap // tm) * tm
    tile_grp = jnp.minimum((tile_start[:, None] >= ends[None, :]).sum(axis=1), N_EXPERTS - 1).astype(jnp.int32)
    rows_used = ends[-1].astype(jnp.int32).reshape(1)
    return tok, tile_grp, rows_used, pos12[:, 0], pos12[:, 1]


def _dense_plan(m, tm, group):
    return jnp.full((m // tm,), group, jnp.int32), jnp.full((1,), m // tm, jnp.int32)


def mixer_tables(n_pad):
    diag, col = attention_tables(n_pad)
    expand = (jnp.arange(LANES)[:, None] == (jnp.arange(SSM_WIDTH) // SSM_HEAD_DIM)[None, :]).astype(BF16)
    return {"attn_diag": diag, "attn_col": col, "expand": expand}


def mixer_layer(h, layer, p, *, bsz, n_pad, tm):
    m, d = h.shape
    lambda_init = 0.8 - 0.6 * math.exp(-0.3 * layer)
    grp, nused = _dense_plan(m, tm, layer)
    mm = functools.partial(grouped_matmul, grp=grp, nused=nused, tm=tm)
    hn = rmsnorm(h, p["mix_norm"][layer])
    mm_in = functools.partial(mm, [hn], [p["mix_w_in_t"]], [0], [], w_transposed=True)
    qkv = mm_in(_epi_plain, ncols=QKV_COLS, tn=512, out_dtype=BF16)
    zx = mm_in(_epi_plain, ncols=ZX_COLS, tn=512, out_dtype=F32, w_col0=(ZX_COL0,))
    dtr = mm_in(_epi_plain, ncols=LANES, tn=LANES, out_dtype=F32, w_col0=(DT_COL0,))
    gates = mm_in(_epi_sigmoid, ncols=2 * d, tn=512, out_dtype=BF16, w_col0=(DT_COL0,),
                  w_shift=(GATE_COL0 - DT_COL0,))
    lam_params = jnp.stack([p["lambda_q1"][layer], p["lambda_k1"][layer], p["lambda_q2"][layer],
                            p["lambda_k2"][layer]]).astype(F32)
    o_attn = diff_attention(qkv, lam_params, p["diff_norm"][layer], p["attn_diag"], p["attn_col"], bsz=bsz,
                            n_pad=n_pad, lambda_init=lambda_init)
    o_ssm = ssd_mixer(zx, dtr, p["conv_w"][layer], p["conv_b"][layer], p["dt_bias"][layer], p["a_log"][layer],
                      p["d_skip"][layer], p["ssm_norm"][layer], p["expand"], bsz=bsz, n_pad=n_pad)
    merged = mm([o_attn, o_ssm], [p["w_proj_attn"], p["w_proj_ssm"]], [0, 1], [gates, gates], _epi_gated_sum,
                ncols=d, tn=512, out_dtype=BF16, extra_col0=(0, d))
    return mm([merged], [p["w_out"]], [0], [h], _epi_residual, ncols=d, tn=512, out_dtype=F32, extra_col0=(0,))


def dense_ffn(h, norm_w, w_gate, w_up, w_down, group, *, tm):
    m, d = h.shape
    d_ff = w_gate.shape[2]
    grp, nused = _dense_plan(m, tm, group)
    mm = functools.partial(grouped_matmul, grp=grp, nused=nused, tm=tm)
    hn = rmsnorm(h, norm_w)
    u = mm([hn], [w_gate, w_up], [0, 0], [], _epi_swiglu, ncols=d_ff, tn=256, out_dtype=BF16)
    tm_down = tm // 2
    grp, nused = _dense_plan(m, tm_down, group)
    return grouped_matmul([u], [w_down.astype(BF16)], [0], [h], _epi_residual, grp=grp, nused=nused, ncols=d,
                          tm=tm_down, tn=512, out_dtype=F32, extra_col0=(0,))


def moe_ffn_final(h, norm_w, router_w, w_gate, w_up, w_down, final_norm, group, *, bsz, n_pad, n_frames):
    d = h.shape[1]
    d_fe = w_gate.shape[3]
    r_pad = jnp.pad(router_w.astype(F32), ((0, 0), (0, LANES - N_EXPERTS)))
    route = route_tokens(h, norm_w, r_pad)
    tok, tile_grp, rows_used, pos1, pos2 = _moe_plan(route, bsz=bsz, n_pad=n_pad, n_frames=n_frames,
                                                      tm=MOE_MM_ROW_TILE)
    xs = moe_dispatch(h, norm_w, tok, rows_used // MOE_ROW_TILE)
    mm = functools.partial(grouped_matmul, grp=tile_grp + group * N_EXPERTS, nused=rows_used // MOE_MM_ROW_TILE,
                           tm=MOE_MM_ROW_TILE)
    u = mm([xs], [w_gate.reshape(-1, d, d_fe), w_up.reshape(-1, d, d_fe)], [0, 0], [], _epi_swiglu,
           ncols=d_fe, tn=512, out_dtype=BF16)
    ys = mm([u], [w_down.reshape(-1, d_fe, d)], [0], [], _epi_plain, ncols=d, tn=1024, out_dtype=F32)
    return moe_combine(ys, h, route, final_norm, pos1, pos2, bsz=bsz, n_pad=n_pad, n_frames=n_frames)


def kernel(x, meta_tokens, mix_norm, mix_w_in, conv_w, conv_b, dt_bias, a_log, d_skip, ssm_norm, lambda_q1, lambda_k1, lambda_q2, lambda_k2, diff_norm, w_proj_attn, w_proj_ssm, w_out, ffn_norm, ffn_w_gate, ffn_w_up, ffn_w_down, router, moe_w_gate, moe_w_up, moe_w_down, final_norm):
    bsz, n_frames, d = x.shape
    assert mix_norm.shape[0] == 2 and d == D_MODEL, "layer 0: dense FFN; layer 1: MoE FFN, then the final norm"
    n = N_META + n_frames
    n_pad = -(-n // Q_BLOCK) * Q_BLOCK
    m = bsz * n_pad
    tm = n_pad // 2
    assert tm % 16 == 0 and n_pad % SSD_BLOCK == 0 and n_frames % MOE_ROW_TILE == 0

    meta = jnp.broadcast_to(meta_tokens[None].astype(x.dtype), (bsz, N_META, d))
    h = jnp.concatenate([meta, x, jnp.zeros((bsz, n_pad - n, d), x.dtype)], axis=1).reshape(m, d)

    p = dict(mix_norm=mix_norm, mix_w_in_t=jnp.swapaxes(mix_w_in, 1, 2), conv_w=conv_w, conv_b=conv_b,
             dt_bias=dt_bias, a_log=a_log,
             d_skip=d_skip, ssm_norm=ssm_norm, lambda_q1=lambda_q1, lambda_k1=lambda_k1, lambda_q2=lambda_q2,
             lambda_k2=lambda_k2, diff_norm=diff_norm, w_proj_attn=w_proj_attn, w_proj_ssm=w_proj_ssm, w_out=w_out)
    p.update(mixer_tables(n_pad))
    h = mixer_layer(h, 0, p, bsz=bsz, n_pad=n_pad, tm=tm)
    h = dense_ffn(h, ffn_norm[0], ffn_w_gate, ffn_w_up, ffn_w_down, 0, tm=tm)
    h = mixer_layer(h, 1, p, bsz=bsz, n_pad=n_pad, tm=tm)
    out = moe_ffn_final(h, ffn_norm[1], router[0], moe_w_gate, moe_w_up, moe_w_down, final_norm, 0,
                        bsz=bsz, n_pad=n_pad, n_frames=n_frames)
    return out.reshape(bsz, n_frames, d)
```

```python
import functools
import math

import jax
import jax.numpy as jnp
from jax import lax
from jax.experimental import pallas as pl
from jax.experimental.pallas import tpu as pltpu

F32 = jnp.float32
BF16 = jnp.bfloat16

LANES = 128
VMEM_LIMIT_BYTES_V7X = 56 * 2**20

D_MODEL = 4096
N_META = 16
CHUNK = 64
Q_BLOCK = 128
ATTN_ROWS = 2 * Q_BLOCK
HEADS = 8
QK_DIM = 128
V_DIM = 256
ATTN_WIDTH = HEADS * V_DIM
SUBLN_EPS = 1e-5
SSM_WIDTH = 2048
SSM_HEAD_DIM = 64
SSM_HEADS = 32
SSM_GROUPS = 4
SSM_STATE = 128
SSM_CONV = 4
SSM_GROUP_WIDTH = SSM_WIDTH // SSM_GROUPS
XBC_WIDTH = SSM_WIDTH + 2 * SSM_GROUPS * SSM_STATE
QKV_COLS = 3 * ATTN_WIDTH
ZX_COL0 = QKV_COLS
ZX_COLS = SSM_WIDTH + XBC_WIDTH
DT_COL0 = ZX_COL0 + ZX_COLS
GATE_COL0 = DT_COL0 + SSM_HEADS
N_EXPERTS = 8
NORM_EPS = 1e-6
NEG_BIG = -1e30
LOG2_E = 1.4426950408889634
SSD_BLOCK = 128

ROW_TILE = 1088
MOE_ROW_TILE = 256
MOE_MM_ROW_TILE = 512
NORM_ROW_TILE = 272


def _silu(x):
    return x * (1.0 / (1.0 + jnp.exp(-x)))


def _sigmoid(x):
    return 1.0 / (1.0 + jnp.exp(-x))


def _compiler_params(semantics):
    return pltpu.CompilerParams(dimension_semantics=semantics, vmem_limit_bytes=VMEM_LIMIT_BYTES_V7X)


def _rmsnorm_kernel(x_ref, w_ref, o_ref, *, eps):
    x = x_ref[...]
    ms = jnp.mean(x * x, axis=-1, keepdims=True)
    o_ref[...] = (x * lax.rsqrt(ms + eps) * w_ref[...]).astype(o_ref.dtype)


def _norm_row_tile(m):
    return NORM_ROW_TILE if m % NORM_ROW_TILE == 0 else Q_BLOCK


def rmsnorm(x, w, *, eps=NORM_EPS, out_dtype=BF16):
    m, d = x.shape
    tr = _norm_row_tile(m)
    return pl.pallas_call(
        functools.partial(_rmsnorm_kernel, eps=eps),
        grid=(m // tr,),
        in_specs=[pl.BlockSpec((tr, d), lambda i: (i, 0)), pl.BlockSpec((1, d), lambda i: (0, 0))],
        out_specs=pl.BlockSpec((tr, d), lambda i: (i, 0)),
        out_shape=jax.ShapeDtypeStruct((m, d), out_dtype),
        compiler_params=_compiler_params(("parallel",)),
    )(x, w.reshape(1, d))


def _split_bf16(x):
    hi = x.astype(BF16)
    lo = (x - hi.astype(F32)).astype(BF16)
    return hi, lo


def _router_kernel(x_ref, w_ref, r_ref, o_ref, *, eps):
    x = x_ref[...]
    ms = jnp.mean(x * x, axis=-1, keepdims=True)
    hn = x * lax.rsqrt(ms + eps) * w_ref[...]
    h_hi, h_lo = _split_bf16(hn)
    r_hi, r_lo = _split_bf16(r_ref[...])
    logits = (jnp.dot(h_hi, r_hi, preferred_element_type=F32)
              + jnp.dot(h_hi, r_lo, preferred_element_type=F32)
              + jnp.dot(h_lo, r_hi, preferred_element_type=F32))
    lane = lax.broadcasted_iota(jnp.int32, logits.shape, 1).astype(F32)
    lg = jnp.where(lane < N_EXPERTS, logits, -jnp.inf)
    m1 = jnp.max(lg, axis=-1, keepdims=True)
    i1 = jnp.min(jnp.where(lg == m1, lane, float(LANES)), axis=-1, keepdims=True)
    lg2 = jnp.where(lane == i1, -jnp.inf, lg)
    m2 = jnp.max(lg2, axis=-1, keepdims=True)
    i2 = jnp.min(jnp.where(lg2 == m2, lane, float(LANES)), axis=-1, keepdims=True)
    e2 = jnp.exp(m2 - m1)
    den = 1.0 + e2
    out = jnp.where(lane == i1, 1.0 / den, 0.0) + jnp.where(lane == i2, e2 / den, 0.0)
    out = jnp.where(lane == N_EXPERTS, i1, out)
    out = jnp.where(lane == N_EXPERTS + 1, i2, out)
    o_ref[...] = out


def route_tokens(x, w, r_pad):
    m, d = x.shape
    tr = _norm_row_tile(m)
    return pl.pallas_call(
        functools.partial(_router_kernel, eps=NORM_EPS),
        grid=(m // tr,),
        in_specs=[pl.BlockSpec((tr, d), lambda i: (i, 0)), pl.BlockSpec((1, d), lambda i: (0, 0)),
                  pl.BlockSpec((d, LANES), lambda i: (0, 0))],
        out_specs=pl.BlockSpec((tr, LANES), lambda i: (i, 0)),
        out_shape=jax.ShapeDtypeStruct((m, LANES), F32),
        compiler_params=_compiler_params(("parallel",)),
    )(x, w.reshape(1, d), r_pad)


def _mm_kernel(grp_ref, nused_ref, *refs, n_a, w_to_a, cast, w_shift, w_transposed, n_extra, nk, epilogue):
    col_axis = 0 if w_transposed else 1
    n_w = len(w_to_a)
    a_refs = refs[:n_a]
    w_refs = refs[n_a:n_a + n_w]
    refs = list(refs[n_a + n_w:])
    wnext_refs = [refs.pop(0) if s else None for s in w_shift]
    e_refs = [refs.pop(0) for _ in range(n_extra)]
    o_ref = refs.pop(0)
    scratch = refs
    wb_refs = [scratch.pop(0) if c else None for c in cast]
    acc_refs = [scratch.pop(0) for _ in range(n_w)] if nk > 1 else []
    i = pl.program_id(1)
    k = pl.program_id(2)

    @pl.when(i >= nused_ref[0])
    def _():
        o_ref[...] = jnp.zeros_like(o_ref)

    @pl.when(i < nused_ref[0])
    def _():
        if any(cast):
            def do_cast():
                for w_ref, wn_ref, wb_ref, s in zip(w_refs, wnext_refs, wb_refs, w_shift):
                    if wb_ref is None:
                        continue
                    if s:
                        tn = w_ref.shape[col_axis]
                        wide = jnp.concatenate([w_ref[...], wn_ref[...]], axis=col_axis)
                        wide = wide[s:s + tn, :] if w_transposed else wide[:, s:s + tn]
                        wb_ref[...] = wide.astype(BF16)
                    else:
                        wb_ref[...] = w_ref[...].astype(BF16)
            if nk == 1:
                changed = jnp.logical_or(i == 0, grp_ref[i] != grp_ref[jnp.maximum(i - 1, 0)])
                pl.when(changed)(do_cast)
            else:
                do_cast()
        parts = []
        for wi in range(n_w):
            w = wb_refs[wi][...] if cast[wi] else w_refs[wi][...]
            contract = (((1,), (1,)), ((), ())) if w_transposed else (((1,), (0,)), ((), ()))
            parts.append(lax.dot_general(a_refs[w_to_a[wi]][...], w, contract, preferred_element_type=F32))
        if nk == 1:
            o_ref[...] = epilogue(parts, e_refs).astype(o_ref.dtype)
        else:
            @pl.when(k == 0)
            def _():
                for acc, p in zip(acc_refs, parts):
                    acc[...] = p

            @pl.when(k > 0)
            def _():
                for acc, p in zip(acc_refs, parts):
                    acc[...] += p

            @pl.when(k == nk - 1)
            def _():
                o_ref[...] = epilogue([acc[...] for acc in acc_refs], e_refs).astype(o_ref.dtype)


def grouped_matmul(a_list, w_list, w_to_a, extras, epilogue, *, grp, nused, ncols, tm, tn, out_dtype,
                   w_col0=None, w_shift=None, w_transposed=False, extra_col0=(), tk=None):
    m, kdim = a_list[0].shape
    tk = kdim if tk is None else tk
    nk = kdim // tk
    w_col0 = (0,) * len(w_list) if w_col0 is None else tuple(w_col0)
    w_shift = (0,) * len(w_list) if w_shift is None else tuple(w_shift)
    assert m % tm == 0 and ncols % tn == 0 and kdim % tk == 0 and all(c % tn == 0 for c in w_col0)
    assert all(a.shape == (m, kdim) for a in a_list)
    assert all(w.shape[2 if w_transposed else 1] == kdim for w in w_list)
    cast = tuple(w.dtype != BF16 for w in w_list)
    assert all(c or not s for c, s in zip(cast, w_shift)) and all(0 <= s < LANES for s in w_shift)
    assert not w_transposed or all(s % 8 == 0 for s in w_shift)
    ecb = tuple(c // tn for c in extra_col0)
    lane_tiles = tn // LANES

    def row(i, n):
        return jnp.minimum(i, n[0] - 1)

    def w_block(width, col_index):
        if w_transposed:
            return pl.BlockSpec((None, width, tk), lambda j, i, k, g, n: (g[row(i, n)], col_index(j), k))
        return pl.BlockSpec((None, tk, width), lambda j, i, k, g, n: (g[row(i, n)], k, col_index(j)))

    def w_spec(cb):
        return w_block(tn, lambda j: cb + j)

    def w_next_spec(cb):
        return w_block(LANES, lambda j: (cb + j + 1) * lane_tiles)

    in_specs = [pl.BlockSpec((tm, tk), lambda j, i, k, g, n: (row(i, n), k)) for _ in a_list]
    in_specs += [w_spec(c // tn) for c in w_col0]
    in_specs += [w_next_spec(c // tn) for c, s in zip(w_col0, w_shift) if s]
    in_specs += [pl.BlockSpec((tm, tn), functools.partial(lambda j, i, k, g, n, eb: (row(i, n), eb + j), eb=eb))
                 for eb in ecb]
    scratch = [pltpu.VMEM((tn, tk) if w_transposed else (tk, tn), BF16) for c in cast if c]
    if nk > 1:
        scratch += [pltpu.VMEM((tm, tn), F32) for _ in w_list]
    kernel = functools.partial(_mm_kernel, n_a=len(a_list), w_to_a=tuple(w_to_a), cast=cast, w_shift=w_shift,
                               w_transposed=w_transposed, n_extra=len(extras), nk=nk, epilogue=epilogue)
    w_next = [w for w, s in zip(w_list, w_shift) if s]
    return pl.pallas_call(
        kernel,
        grid_spec=pltpu.PrefetchScalarGridSpec(
            num_scalar_prefetch=2,
            grid=(ncols // tn, m // tm, nk),
            in_specs=in_specs,
            out_specs=pl.BlockSpec((tm, tn), lambda j, i, k, g, n: (i, j)),
            scratch_shapes=scratch),
        out_shape=jax.ShapeDtypeStruct((m, ncols), out_dtype),
        compiler_params=_compiler_params(("arbitrary", "arbitrary", "arbitrary")),
    )(grp, nused, *a_list, *w_list, *w_next, *extras)


def _epi_plain(parts, e_refs):
    return parts[0]


def _epi_sigmoid(parts, e_refs):
    return _sigmoid(parts[0])


def _epi_gated_sum(parts, e_refs):
    return e_refs[0][...] * parts[0] + e_refs[1][...] * parts[1]


def _epi_residual(parts, e_refs):
    return e_refs[0][...] + parts[0]


def _epi_swiglu(parts, e_refs):
    return _silu(parts[0]) * parts[1]


def attention_tables(n_pad):
    slopes = jnp.exp2(-8.0 * (jnp.arange(HEADS, dtype=F32) + 1.0) / HEADS) * LOG2_E
    i = jnp.arange(ATTN_ROWS)[:, None]
    j = jnp.arange(ATTN_ROWS + Q_BLOCK)[None, :]
    visible = (i - N_META) // CHUNK >= (j - N_META) // CHUNK
    rel = (i - jnp.abs(i - j)).astype(F32)
    diag = jnp.where(visible[None], slopes[:, None, None] * rel[None], NEG_BIG)
    col = slopes[:, None, None] * jnp.arange(n_pad, dtype=F32)[None, None, :]
    return diag, col


def _attn_kernel(q_ref, k_ref, v_ref, lam_ref, dn_ref, diag_ref, col_ref, o_ref, kt_ref, *, lambda_init):
    n_pad = q_ref.shape[0]
    lp = lam_ref[...]
    s1 = jnp.sum(lp[0:1] * lp[1:2], axis=-1, keepdims=True)
    s2 = jnp.sum(lp[2:3] * lp[3:4], axis=-1, keepdims=True)
    lam = jnp.exp(s1) - jnp.exp(s2) + lambda_init
    for mp in range(2):
        kt_ref[mp] = k_ref[:, mp * QK_DIM:(mp + 1) * QK_DIM].T
    scale2 = QK_DIM ** -0.5 * LOG2_E
    gain = dn_ref[...] * (1.0 - lambda_init)
    for q0 in range(0, n_pad, ATTN_ROWS):
        qb = q0 // ATTN_ROWS
        n_rows = min(ATTN_ROWS, n_pad - q0)
        width = min(n_rows + Q_BLOCK, n_pad - q0)
        rows = slice(q0, q0 + n_rows)
        diag_bias = diag_ref[:n_rows, :width]
        past_bias = col_ref[:, :q0] - col_ref[:, q0:q0 + 1] if qb else None
        heads = []
        for mp in range(2):
            q = q_ref[rows, mp * QK_DIM:(mp + 1) * QK_DIM]
            s_d = jnp.dot(q, kt_ref[mp, :, q0:q0 + width], preferred_element_type=F32) * scale2 + diag_bias
            mx = jnp.max(s_d, axis=-1, keepdims=True)
            if qb:
                s_p = jnp.dot(q, kt_ref[mp, :, :q0], preferred_element_type=F32) * scale2 + past_bias
                mx = jnp.maximum(mx, jnp.max(s_p, axis=-1, keepdims=True))
            e_d = jnp.exp2(s_d - mx)
            den = jnp.sum(e_d, axis=-1, keepdims=True)
            pv = jnp.dot(e_d.astype(BF16), v_ref[q0:q0 + width, :], preferred_element_type=F32)
            if qb:
                e_p = jnp.exp2(s_p - mx)
                den = den + jnp.sum(e_p, axis=-1, keepdims=True)
                pv = pv + jnp.dot(e_p.astype(BF16), v_ref[:q0, :], preferred_element_type=F32)
            heads.append(pv / den)
        o = heads[0] - lam * heads[1]
        ms = jnp.mean(o * o, axis=-1, keepdims=True)
        o_ref[rows, :] = (o * lax.rsqrt(ms + SUBLN_EPS) * gain).astype(o_ref.dtype)


def diff_attention(qkv, lam_params, diff_norm, diag, col, *, bsz, n_pad, lambda_init):
    m = qkv.shape[0]
    kernel = functools.partial(_attn_kernel, lambda_init=lambda_init)
    return pl.pallas_call(
        kernel,
        grid=(bsz, HEADS),
        in_specs=[
            pl.BlockSpec((n_pad, V_DIM), lambda b, h: (b, h)),
            pl.BlockSpec((n_pad, V_DIM), lambda b, h: (b, HEADS + h)),
            pl.BlockSpec((n_pad, V_DIM), lambda b, h: (b, 2 * HEADS + h)),
            pl.BlockSpec((4, QK_DIM), lambda b, h: (0, 0)),
            pl.BlockSpec((1, V_DIM), lambda b, h: (0, 0)),
            pl.BlockSpec((None, ATTN_ROWS, ATTN_ROWS + Q_BLOCK), lambda b, h: (h, 0, 0)),
            pl.BlockSpec((None, 1, n_pad), lambda b, h: (h, 0, 0)),
        ],
        out_specs=pl.BlockSpec((n_pad, V_DIM), lambda b, h: (b, h)),
        out_shape=jax.ShapeDtypeStruct((m, ATTN_WIDTH), BF16),
        scratch_shapes=[pltpu.VMEM((2, QK_DIM, n_pad), BF16)],
        compiler_params=_compiler_params(("parallel", "parallel")),
    )(qkv, qkv, qkv, lam_params, diff_norm.reshape(1, V_DIM), diag, col)


def _split3_bf16(x):
    hi = x.astype(BF16)
    r1 = x - hi.astype(F32)
    mid = r1.astype(BF16)
    lo = (r1 - mid.astype(F32)).astype(BF16)
    return hi, mid, lo


def _ssd_kernel(zx_ref, dtr_ref, cw_ref, cb_ref, dtb_ref, alog_ref, dsk_ref, nw_ref, ex_ref, o_ref,
                state_ref, xbuf_ref, y_ref):
    c = pl.program_id(1)
    L = SSD_BLOCK
    halo = 8

    @pl.when(c == 0)
    def _():
        state_ref[...] = jnp.zeros_like(state_ref)
        xbuf_ref[0:halo, :] = jnp.zeros((halo, XBC_WIDTH), F32)

    @pl.when(c > 0)
    def _():
        xbuf_ref[0:halo, :] = xbuf_ref[L:L + halo, :]

    u = zx_ref[:, SSM_WIDTH:SSM_WIDTH + XBC_WIDTH]
    xbuf_ref[halo:halo + L, :] = u
    acc = cb_ref[...] + cw_ref[SSM_CONV - 1:SSM_CONV, :] * u
    for s in range(1, SSM_CONV):
        acc = acc + cw_ref[SSM_CONV - 1 - s:SSM_CONV - s, :] * xbuf_ref[halo - s:halo - s + L, :]
    xbc = _silu(acc)
    xs = xbc[:, :SSM_WIDTH]
    bm = xbc[:, SSM_WIDTH:SSM_WIDTH + SSM_GROUPS * SSM_STATE].astype(BF16)
    cm = xbc[:, SSM_WIDTH + SSM_GROUPS * SSM_STATE:].astype(BF16)

    pre = dtr_ref[...] + dtb_ref[...]
    dt = jnp.maximum(pre, 0.0) + jnp.log(1.0 + jnp.exp(-jnp.abs(pre)))
    adt = dt * (-jnp.exp(alog_ref[...]))
    rows = lax.broadcasted_iota(jnp.int32, (L, LANES), 0)
    cum = adt
    shift = 1
    while shift < L:
        cum = cum + jnp.where(rows >= shift, pltpu.roll(cum, shift, 0), 0.0)
        shift *= 2
    cum_last = cum[L - 1:L, :]
    ecum = jnp.exp(cum)
    edec = jnp.exp(cum_last - cum)
    elast = jnp.broadcast_to(jnp.exp(cum_last), (8, LANES))
    stacked = jnp.concatenate([dt, ecum, edec, elast], axis=0)
    ex = ex_ref[...]
    wide = sum(jnp.dot(p, ex, preferred_element_type=F32) for p in _split3_bf16(stacked))
    dt_w, ecum_w, edec_w, elast_w = wide[0:L], wide[L:2 * L], wide[2 * L:3 * L], wide[3 * L:3 * L + 1]

    xd = xs * dt_w
    xd_b = xd.astype(BF16)
    xdec_b = (xd * edec_w).astype(BF16)
    cum_t = cum.T
    tril = lax.broadcasted_iota(jnp.int32, (L, L), 0) >= lax.broadcasted_iota(jnp.int32, (L, L), 1)
    heads_per_group = SSM_HEADS // SSM_GROUPS
    for g in range(SSM_GROUPS):
        gs = slice(g * SSM_GROUP_WIDTH, (g + 1) * SSM_GROUP_WIDTH)
        bg = bm[:, g * SSM_STATE:(g + 1) * SSM_STATE]
        cg = cm[:, g * SSM_STATE:(g + 1) * SSM_STATE]
        cb = lax.dot_general(cg, bg, (((1,), (1,)), ((), ())), preferred_element_type=F32)
        st = state_ref[:, gs]
        y_off = jnp.dot(cg, st.astype(BF16), preferred_element_type=F32) * ecum_w[:, gs]
        state_ref[:, gs] = elast_w[:, gs] * st + lax.dot_general(
            bg, xdec_b[:, gs], (((0,), (0,)), ((), ())), preferred_element_type=F32)
        y_ref[:, gs] = y_off
        for r in range(heads_per_group):
            hd = g * heads_per_group + r
            hs = slice(hd * SSM_HEAD_DIM, (hd + 1) * SSM_HEAD_DIM)
            seg = cum[:, hd:hd + 1] - cum_t[hd:hd + 1, :]
            lmat = jnp.exp(jnp.where(tril, seg, -jnp.inf))
            y_ref[:, hs] += jnp.dot((cb * lmat).astype(BF16), xd_b[:, hs], preferred_element_type=F32)

    y = y_ref[...] + dsk_ref[...] * xs
    gated = y * _silu(zx_ref[:, 0:SSM_WIDTH])
    for g in range(SSM_GROUPS):
        gs = slice(g * SSM_GROUP_WIDTH, (g + 1) * SSM_GROUP_WIDTH)
        gg = gated[:, gs]
        ms = jnp.mean(gg * gg, axis=-1, keepdims=True)
        o_ref[:, gs] = (gg * lax.rsqrt(ms + NORM_EPS) * nw_ref[:, gs]).astype(o_ref.dtype)


def ssd_mixer(zx, dtr, conv_w, conv_b, dt_bias, a_log, d_skip, ssm_norm, expand, *, bsz, n_pad):
    m = zx.shape[0]
    nc = n_pad // SSD_BLOCK
    L = SSD_BLOCK

    def pad_heads(v):
        return jnp.pad(v.astype(F32), (0, LANES - SSM_HEADS)).reshape(1, LANES)

    def const(shape):
        return pl.BlockSpec(shape, lambda b, c: (0, 0))

    return pl.pallas_call(
        _ssd_kernel,
        grid=(bsz, nc),
        in_specs=[
            pl.BlockSpec((L, ZX_COLS), lambda b, c: (b * nc + c, 0)),
            pl.BlockSpec((L, LANES), lambda b, c: (b * nc + c, 0)),
            const((SSM_CONV, XBC_WIDTH)), const((1, XBC_WIDTH)), const((1, LANES)), const((1, LANES)),
            const((1, SSM_WIDTH)), const((1, SSM_WIDTH)), const((LANES, SSM_WIDTH)),
        ],
        out_specs=pl.BlockSpec((L, SSM_WIDTH), lambda b, c: (b * nc + c, 0)),
        out_shape=jax.ShapeDtypeStruct((m, SSM_WIDTH), BF16),
        scratch_shapes=[pltpu.VMEM((SSM_STATE, SSM_WIDTH), F32), pltpu.VMEM((L + 8, XBC_WIDTH), F32),
                        pltpu.VMEM((L, SSM_WIDTH), F32)],
        compiler_params=_compiler_params(("parallel", "arbitrary")),
    )(zx, dtr, conv_w, conv_b.reshape(1, XBC_WIDTH), pad_heads(dt_bias), pad_heads(a_log),
      jnp.repeat(d_skip.astype(F32), SSM_HEAD_DIM).reshape(1, SSM_WIDTH), ssm_norm.reshape(1, SSM_WIDTH), expand)


def _dispatch_kernel(tok_ref, nused_ref, h_hbm, w_ref, o_ref, buf_ref, sem):
    i = pl.program_id(0)
    g = buf_ref.shape[1]
    nused = nused_ref[0]
    slot = i % 2

    def row_copy(src_row, dst_slot, r):
        return pltpu.make_async_copy(h_hbm.at[pl.ds(src_row, 1)], buf_ref.at[dst_slot, pl.ds(r, 1)],
                                     sem.at[dst_slot])

    def start_tile(tile, dst_slot):
        def body(r, carry):
            row_copy(tok_ref[tile * g + r], dst_slot, r).start()
            return carry
        lax.fori_loop(0, g, body, 0, unroll=8)

    def wait_tile(dst_slot):
        def body(r, carry):
            row_copy(0, dst_slot, r).wait()
            return carry
        lax.fori_loop(0, g, body, 0, unroll=8)

    @pl.when(jnp.logical_and(i == 0, nused > 0))
    def _():
        start_tile(0, 0)

    @pl.when(i + 1 < nused)
    def _():
        start_tile(i + 1, 1 - slot)

    @pl.when(i < nused)
    def _():
        wait_tile(slot)
        x = buf_ref[slot]
        ms = jnp.mean(x * x, axis=-1, keepdims=True)
        o_ref[...] = (x * lax.rsqrt(ms + NORM_EPS) * w_ref[...]).astype(o_ref.dtype)

    @pl.when(i >= nused)
    def _():
        o_ref[...] = jnp.zeros_like(o_ref)


def moe_dispatch(h, w, tok, nused, *, tg=MOE_ROW_TILE):
    d = h.shape[1]
    rows = tok.shape[0]
    return pl.pallas_call(
        _dispatch_kernel,
        grid_spec=pltpu.PrefetchScalarGridSpec(
            num_scalar_prefetch=2,
            grid=(rows // tg,),
            in_specs=[pl.BlockSpec(memory_space=pl.ANY), pl.BlockSpec((1, d), lambda i, t, n: (0, 0))],
            out_specs=pl.BlockSpec((tg, d), lambda i, t, n: (i, 0)),
            scratch_shapes=[pltpu.VMEM((2, tg, d), F32), pltpu.SemaphoreType.DMA((2,))]),
        out_shape=jax.ShapeDtypeStruct((rows, d), BF16),
        compiler_params=_compiler_params(("arbitrary",)),
    )(tok, nused, h, w.reshape(1, d))


def _combine_kernel(pos1_ref, pos2_ref, y_hbm, h_hbm, route_hbm, w_ref, o_ref,
                    y1_ref, y2_ref, h_ref, r_ref, sem_y, sem_h, *, n_pad, n_frames):
    i = pl.program_id(0)
    n_tiles = pl.num_programs(0)
    tr = h_ref.shape[1]
    tiles_per_batch = n_frames // tr
    slot = i % 2

    def block_copies(tile, dst_slot):
        b = tile // tiles_per_batch
        row0 = b * n_pad + N_META + (tile - b * tiles_per_batch) * tr
        return (pltpu.make_async_copy(h_hbm.at[pl.ds(row0, tr)], h_ref.at[dst_slot], sem_h.at[0, dst_slot]),
                pltpu.make_async_copy(route_hbm.at[pl.ds(row0, tr)], r_ref.at[dst_slot], sem_h.at[1, dst_slot]))

    def y_copies(row1, row2, dst_slot, r):
        return (pltpu.make_async_copy(y_hbm.at[pl.ds(row1, 1)], y1_ref.at[dst_slot, pl.ds(r, 1)], sem_y.at[dst_slot]),
                pltpu.make_async_copy(y_hbm.at[pl.ds(row2, 1)], y2_ref.at[dst_slot, pl.ds(r, 1)], sem_y.at[dst_slot]))

    def start_tile(tile, dst_slot):
        for cp in block_copies(tile, dst_slot):
            cp.start()

        def body(r, carry):
            t = tile * tr + r
            for cp in y_copies(pos1_ref[t], pos2_ref[t], dst_slot, r):
                cp.start()
            return carry
        lax.fori_loop(0, tr, body, 0, unroll=8)

    def wait_tile(tile, dst_slot):
        for cp in block_copies(tile, dst_slot):
            cp.wait()

        def body(r, carry):
            for cp in y_copies(0, 0, dst_slot, r):
                cp.wait()
            return carry
        lax.fori_loop(0, tr, body, 0, unroll=8)

    @pl.when(i == 0)
    def _():
        start_tile(0, 0)

    @pl.when(i + 1 < n_tiles)
    def _():
        start_tile(i + 1, 1 - slot)

    wait_tile(i, slot)
    route = r_ref[slot]
    lane = lax.broadcasted_iota(jnp.int32, route.shape, 1).astype(F32)
    in_experts = lane < N_EXPERTS
    w1 = jnp.sum(jnp.where(jnp.logical_and(in_experts, lane == route[:, N_EXPERTS:N_EXPERTS + 1]), route, 0.0),
                 axis=-1, keepdims=True)
    w2 = jnp.sum(jnp.where(jnp.logical_and(in_experts, lane == route[:, N_EXPERTS + 1:N_EXPERTS + 2]), route, 0.0),
                 axis=-1, keepdims=True)
    x = h_ref[slot] + w1 * y1_ref[slot] + w2 * y2_ref[slot]
    ms = jnp.mean(x * x, axis=-1, keepdims=True)
    o_ref[...] = x * lax.rsqrt(ms + NORM_EPS) * w_ref[...]


def moe_combine(ys, h, route, final_norm, pos1, pos2, *, bsz, n_pad, n_frames, tr=MOE_ROW_TILE):
    d = h.shape[1]
    kernel = functools.partial(_combine_kernel, n_pad=n_pad, n_frames=n_frames)
    any_spec = pl.BlockSpec(memory_space=pl.ANY)
    return pl.pallas_call(
        kernel,
        grid_spec=pltpu.PrefetchScalarGridSpec(
            num_scalar_prefetch=2,
            grid=(bsz * n_frames // tr,),
            in_specs=[any_spec, any_spec, any_spec, pl.BlockSpec((1, d), lambda i, p1, p2: (0, 0))],
            out_specs=pl.BlockSpec((tr, d), lambda i, p1, p2: (i, 0)),
            scratch_shapes=[pltpu.VMEM((2, tr, d), F32), pltpu.VMEM((2, tr, d), F32), pltpu.VMEM((2, tr, d), F32),
                            pltpu.VMEM((2, tr, LANES), F32), pltpu.SemaphoreType.DMA((2,)),
                            pltpu.SemaphoreType.DMA((2, 2))]),
        out_shape=jax.ShapeDtypeStruct((bsz * n_frames, d), F32),
        compiler_params=_compiler_params(("arbitrary",)),
    )(pos1, pos2, ys, h, route, final_norm.reshape(1, d))


def _moe_plan(route, *, bsz, n_pad, n_frames, tm):
    n_tok = bsz * n_frames
    cap = 2 * n_tok + N_EXPERTS * tm
    frames = route.reshape(bsz, n_pad, LANES)[:, N_META:N_META + n_frames].reshape(n_tok, LANES)
    idx = frames[:, N_EXPERTS:N_EXPERTS + 2].astype(jnp.int32)
    sel = (idx[:, :, None] == jnp.arange(N_EXPERTS)[None, None, :]).any(axis=1)
    counts = sel.sum(axis=0)
    padded = ((counts + tm - 1) // tm) * tm
    ends = jnp.cumsum(padded)
    starts = ends - padded
    rank = jnp.cumsum(sel, axis=0) - 1
    pos = starts[None, :] + rank
    pos12 = jnp.take_along_axis(pos, idx, axis=1).astype(jnp.int32)
    tok_rows = (jnp.arange(n_tok) // n_frames) * n_pad + N_META + jnp.arange(n_tok) % n_frames
    tok = jnp.zeros((cap,), jnp.int32).at[pos12.reshape(-1)].set(
        jnp.repeat(tok_rows.astype(jnp.int32), 2), mode="drop")
    tile_start = jnp.arange(cap // tm) * tm
    tile_grp = jnp.minimum((tile_start[:, None] >= ends[None, :]).sum(axis=1), N_EXPERTS - 1).astype(jnp.int32)
    rows_used = ends[-1].astype(jnp.int32).reshape(1)
    return tok, tile_grp, rows_used, pos12[:, 0], pos12[:, 1]


def _dense_plan(m, tm, group):
    return jnp.full((m // tm,), group, jnp.int32), jnp.full((1,), m // tm, jnp.int32)


def mixer_tables(n_pad):
    diag, col = attention_tables(n_pad)
    expand = (jnp.arange(LANES)[:, None] == (jnp.arange(SSM_WIDTH) // SSM_HEAD_DIM)[None, :]).astype(BF16)
    return {"attn_diag": diag, "attn_col": col, "expand": expand}


def mixer_layer(h, layer, p, *, bsz, n_pad, tm):
    m, d = h.shape
    lambda_init = 0.8 - 0.6 * math.exp(-0.3 * layer)
    grp, nused = _dense_plan(m, tm, layer)
    mm = functools.partial(grouped_matmul, grp=grp, nused=nused, tm=tm)
    hn = rmsnorm(h, p["mix_norm"][layer])
    mm_in = functools.partial(mm, [hn], [p["mix_w_in_t"]], [0], [], w_transposed=True)
    qkv = mm_in(_epi_plain, ncols=QKV_COLS, tn=512, out_dtype=BF16)
    zx = mm_in(_epi_plain, ncols=ZX_COLS, tn=512, out_dtype=F32, w_col0=(ZX_COL0,))
    dtr = mm_in(_epi_plain, ncols=LANES, tn=LANES, out_dtype=F32, w_col0=(DT_COL0,))
    gates = mm_in(_epi_sigmoid, ncols=2 * d, tn=512, out_dtype=BF16, w_col0=(DT_COL0,),
                  w_shift=(GATE_COL0 - DT_COL0,))
    lam_params = jnp.stack([p["lambda_q1"][layer], p["lambda_k1"][layer], p["lambda_q2"][layer],
                            p["lambda_k2"][layer]]).astype(F32)
    o_attn = diff_attention(qkv, lam_params, p["diff_norm"][layer], p["attn_diag"], p["attn_col"], bsz=bsz,
                            n_pad=n_pad, lambda_init=lambda_init)
    o_ssm = ssd_mixer(zx, dtr, p["conv_w"][layer], p["conv_b"][layer], p["dt_bias"][layer], p["a_log"][layer],
                      p["d_skip"][layer], p["ssm_norm"][layer], p["expand"], bsz=bsz, n_pad=n_pad)
    merged = mm([o_attn, o_ssm], [p["w_proj_attn"], p["w_proj_ssm"]], [0, 1], [gates, gates], _epi_gated_sum,
                ncols=d, tn=512, out_dtype=BF16, extra_col0=(0, d))
    return mm([merged], [p["w_out"]], [0], [h], _epi_residual, ncols=d, tn=512, out_dtype=F32, extra_col0=(0,))


def dense_ffn(h, norm_w, w_gate, w_up, w_down, group, *, tm):
    m, d = h.shape
    d_ff = w_gate.shape[2]
    grp, nused = _dense_plan(m, tm, group)
    mm = functools.partial(grouped_matmul, grp=grp, nused=nused, tm=tm)
    hn = rmsnorm(h, norm_w)
    u = mm([hn], [w_gate, w_up], [0, 0], [], _epi_swiglu, ncols=d_ff, tn=256, out_dtype=BF16)
    tm_down = tm // 2
    grp, nused = _dense_plan(m, tm_down, group)
    return grouped_matmul([u], [w_down.astype(BF16)], [0], [h], _epi_residual, grp=grp, nused=nused, ncols=d,
                          tm=tm_down, tn=512, out_dtype=F32, extra_col0=(0,))


def moe_ffn_final(h, norm_w, router_w, w_gate, w_up, w_down, final_norm, group, *, bsz, n_pad, n_frames):
    d = h.shape[1]
    d_fe = w_gate.shape[3]
    r_pad = jnp.pad(router_w.astype(F32), ((0, 0), (0, LANES - N_EXPERTS)))
    route = route_tokens(h, norm_w, r_pad)
    tok, tile_grp, rows_used, pos1, pos2 = _moe_plan(route, bsz=bsz, n_pad=n_pad, n_frames=n_frames,
                                                      tm=MOE_MM_ROW_TILE)
    xs = moe_dispatch(h, norm_w, tok, rows_used // MOE_ROW_TILE)
    mm = functools.partial(grouped_matmul, grp=tile_grp + group * N_EXPERTS, nused=rows_used // MOE_MM_ROW_TILE,
                           tm=MOE_MM_ROW_TILE)
    u = mm([xs], [w_gate.reshape(-1, d, d_fe), w_up.reshape(-1, d, d_fe)], [0, 0], [], _epi_swiglu,
           ncols=d_fe, tn=512, out_dtype=BF16)
    ys = mm([u], [w_down.reshape(-1, d_fe, d)], [0], [], _epi_plain, ncols=d, tn=1024, out_dtype=F32)
    return moe_combine(ys, h, route, final_norm, pos1, pos2, bsz=bsz, n_pad=n_pad, n_frames=n_frames)


def kernel(x, meta_tokens, mix_norm, mix_w_in, conv_w, conv_b, dt_bias, a_log, d_skip, ssm_norm, lambda_q1, lambda_k1, lambda_q2, lambda_k2, diff_norm, w_proj_attn, w_proj_ssm, w_out, ffn_norm, ffn_w_gate, ffn_w_up, ffn_w_down, router, moe_w_gate, moe_w_up, moe_w_down, final_norm):
    bsz, n_frames, d = x.shape
    assert mix_norm.shape[0] == 2 and d == D_MODEL, "layer 0: dense FFN; layer 1: MoE FFN, then the final norm"
    n = N_META + n_frames
    n_pad = -(-n // Q_BLOCK) * Q_BLOCK
    m = bsz * n_pad
    tm = n_pad // 2
    assert tm % 16 == 0 and n_pad % SSD_BLOCK == 0 and n_frames % MOE_ROW_TILE == 0

    meta = jnp.broadcast_to(meta_tokens[None].astype(x.dtype), (bsz, N_META, d))
    h = jnp.concatenate([meta, x, jnp.zeros((bsz, n_pad - n, d), x.dtype)], axis=1).reshape(m, d)

    p = dict(mix_norm=mix_norm, mix_w_in_t=jnp.swapaxes(mix_w_in, 1, 2), conv_w=conv_w, conv_b=conv_b,
             dt_bias=dt_bias, a_log=a_log,
             d_skip=d_skip, ssm_norm=ssm_norm, lambda_q1=lambda_q1, lambda_k1=lambda_k1, lambda_q2=lambda_q2,
             lambda_k2=lambda_k2, diff_norm=diff_norm, w_proj_attn=w_proj_attn, w_proj_ssm=w_proj_ssm, w_out=w_out)
    p.update(mixer_tables(n_pad))
    h = mixer_layer(h, 0, p, bsz=bsz, n_pad=n_pad, tm=tm)
    h = dense_ffn(h, ffn_norm[0], ffn_w_gate, ffn_w_up, ffn_w_down, 0, tm=tm)
    h = mixer_layer(h, 1, p, bsz=bsz, n_pad=n_pad, tm=tm)
    out = moe_ffn_final(h, ffn_norm[1], router[0], moe_w_gate, moe_w_up, moe_w_down, final_norm, 0,
                        bsz=bsz, n_pad=n_pad, n_frames=n_frames)
    return out.reshape(bsz, n_frames, d)
```

```python
import functools
import math

import jax
import jax.numpy as jnp
import numpy as np
from jax import lax
from jax.experimental import pallas as pl
from jax.experimental.pallas import tpu as pltpu

F32 = jnp.float32
BF16 = jnp.bfloat16

LANES = 128
VMEM_LIMIT_BYTES_V7X = 56 * 2**20

D_MODEL = 4096
N_META = 16
CHUNK = 64
Q_BLOCK = 128
ATTN_ROWS = 2 * Q_BLOCK
HEADS = 8
QK_DIM = 128
V_DIM = 256
ATTN_WIDTH = HEADS * V_DIM
SUBLN_EPS = 1e-5
SSM_WIDTH = 2048
SSM_HEAD_DIM = 64
SSM_HEADS = 32
SSM_GROUPS = 4
SSM_STATE = 128
SSM_CONV = 4
SSM_GROUP_WIDTH = SSM_WIDTH // SSM_GROUPS
XBC_WIDTH = SSM_WIDTH + 2 * SSM_GROUPS * SSM_STATE
QKV_COLS = 3 * ATTN_WIDTH
ZX_COL0 = QKV_COLS
ZX_COLS = SSM_WIDTH + XBC_WIDTH
DT_COL0 = ZX_COL0 + ZX_COLS
GATE_COL0 = DT_COL0 + SSM_HEADS
N_EXPERTS = 8
NORM_EPS = 1e-6
NEG_BIG = -1e30
LOG2_E = 1.4426950408889634
SSD_BLOCK = 128

ROW_TILE = 1088
MOE_ROW_TILE = 256
MOE_MM_ROW_TILE = 512
NORM_ROW_TILE = 272


def _silu(x):
    return x * (1.0 / (1.0 + jnp.exp(-x)))


def _sigmoid(x):
    return 1.0 / (1.0 + jnp.exp(-x))


def _compiler_params(semantics):
    return pltpu.CompilerParams(dimension_semantics=semantics, vmem_limit_bytes=VMEM_LIMIT_BYTES_V7X)


def _rmsnorm_kernel(x_ref, w_ref, o_ref, *, eps):
    x = x_ref[...]
    ms = jnp.mean(x * x, axis=-1, keepdims=True)
    o_ref[...] = (x * lax.rsqrt(ms + eps) * w_ref[...]).astype(o_ref.dtype)


def _norm_row_tile(m):
    return NORM_ROW_TILE if m % NORM_ROW_TILE == 0 else Q_BLOCK


def _layer_spec(rows, width, layer):
    return pl.BlockSpec((None, rows, width), lambda *_: (layer, 0, 0))


def rmsnorm(x, w_stack, layer, *, eps=NORM_EPS, out_dtype=BF16):
    m, d = x.shape
    tr = _norm_row_tile(m)
    return pl.pallas_call(
        functools.partial(_rmsnorm_kernel, eps=eps),
        grid=(m // tr,),
        in_specs=[pl.BlockSpec((tr, d), lambda i: (i, 0)), _layer_spec(1, d, layer)],
        out_specs=pl.BlockSpec((tr, d), lambda i: (i, 0)),
        out_shape=jax.ShapeDtypeStruct((m, d), out_dtype),
        compiler_params=_compiler_params(("parallel",)),
    )(x, w_stack)


def _embed_norm_kernel(x_ref, meta_ref, w_ref, h_ref, hn_ref, *, eps):
    c = pl.program_id(1)
    last = pl.num_programs(1) - 1
    tail = Q_BLOCK - N_META

    def emit(rows):
        h_ref[...] = rows
        ms = jnp.mean(rows * rows, axis=-1, keepdims=True)
        hn_ref[...] = (rows * lax.rsqrt(ms + eps) * w_ref[...]).astype(hn_ref.dtype)

    @pl.when(c == 0)
    def _():
        emit(jnp.concatenate([meta_ref[...], x_ref[0, 0:tail, :]], axis=0))

    @pl.when(jnp.logical_and(c > 0, c < last))
    def _():
        emit(x_ref[0])

    @pl.when(c == last)
    def _():
        emit(jnp.concatenate([x_ref[0, tail:Q_BLOCK, :], jnp.zeros((tail, x_ref.shape[2]), F32)], axis=0))


def embed_norm(x, meta_tokens, w_stack, layer):
    bsz, n_frames, d = x.shape
    assert n_frames % Q_BLOCK == 0 and meta_tokens.shape == (N_META, d)
    nc = n_frames // Q_BLOCK + 1
    m = bsz * nc * Q_BLOCK

    def x_index(b, c):
        start = jnp.clip(c * Q_BLOCK - N_META, 0, n_frames - Q_BLOCK)
        return b, pl.multiple_of(start, N_META), 0

    row_spec = pl.BlockSpec((Q_BLOCK, d), lambda b, c: (b * nc + c, 0))
    return pl.pallas_call(
        functools.partial(_embed_norm_kernel, eps=NORM_EPS),
        grid=(bsz, nc),
        in_specs=[pl.BlockSpec((pl.Element(1), pl.Element(Q_BLOCK), pl.Element(d)), x_index),
                  pl.BlockSpec((N_META, d), lambda b, c: (0, 0)), _layer_spec(1, d, layer)],
        out_specs=[row_spec, row_spec],
        out_shape=[jax.ShapeDtypeStruct((m, d), F32), jax.ShapeDtypeStruct((m, d), BF16)],
        compiler_params=_compiler_params(("parallel", "arbitrary")),
    )(x, meta_tokens.astype(F32), w_stack)


def _split_bf16(x):
    hi = x.astype(BF16)
    lo = (x - hi.astype(F32)).astype(BF16)
    return hi, lo


def _router_kernel(x_ref, w_ref, r_ref, o_ref, *, eps):
    x = x_ref[...]
    ms = jnp.mean(x * x, axis=-1, keepdims=True)
    hn = x * lax.rsqrt(ms + eps) * w_ref[...]
    h_hi, h_lo = _split_bf16(hn)
    r_hi, r_lo = _split_bf16(r_ref[...])
    logits = (jnp.dot(h_hi, r_hi, preferred_element_type=F32)
              + jnp.dot(h_hi, r_lo, preferred_element_type=F32)
              + jnp.dot(h_lo, r_hi, preferred_element_type=F32))
    lane = lax.broadcasted_iota(jnp.int32, logits.shape, 1).astype(F32)
    lg = jnp.where(lane < N_EXPERTS, logits, -jnp.inf)
    m1 = jnp.max(lg, axis=-1, keepdims=True)
    i1 = jnp.min(jnp.where(lg == m1, lane, float(LANES)), axis=-1, keepdims=True)
    lg2 = jnp.where(lane == i1, -jnp.inf, lg)
    m2 = jnp.max(lg2, axis=-1, keepdims=True)
    i2 = jnp.min(jnp.where(lg2 == m2, lane, float(LANES)), axis=-1, keepdims=True)
    e2 = jnp.exp(m2 - m1)
    den = 1.0 + e2
    out = jnp.where(lane == i1, 1.0 / den, 0.0) + jnp.where(lane == i2, e2 / den, 0.0)
    out = jnp.where(lane == N_EXPERTS, i1, out)
    out = jnp.where(lane == N_EXPERTS + 1, i2, out)
    o_ref[...] = out


def route_tokens(x, w_stack, layer, r_pad):
    m, d = x.shape
    tr = _norm_row_tile(m)
    return pl.pallas_call(
        functools.partial(_router_kernel, eps=NORM_EPS),
        grid=(m // tr,),
        in_specs=[pl.BlockSpec((tr, d), lambda i: (i, 0)), _layer_spec(1, d, layer),
                  pl.BlockSpec((d, LANES), lambda i: (0, 0))],
        out_specs=pl.BlockSpec((tr, LANES), lambda i: (i, 0)),
        out_shape=jax.ShapeDtypeStruct((m, LANES), F32),
        compiler_params=_compiler_params(("parallel",)),
    )(x, w_stack, r_pad)


def _mm_kernel(grp_ref, nused_ref, *refs, n_a, w_to_a, cast, w_shift, w_transposed, n_extra, nk, epilogue):
    col_axis = 0 if w_transposed else 1
    n_w = len(w_to_a)
    a_refs = refs[:n_a]
    w_refs = refs[n_a:n_a + n_w]
    refs = list(refs[n_a + n_w:])
    wnext_refs = [refs.pop(0) if s else None for s in w_shift]
    e_refs = [refs.pop(0) for _ in range(n_extra)]
    o_ref = refs.pop(0)
    scratch = refs
    wb_refs = [scratch.pop(0) if c else None for c in cast]
    acc_refs = [scratch.pop(0) for _ in range(n_w)] if nk > 1 else []
    i = pl.program_id(1)
    k = pl.program_id(2)

    @pl.when(i >= nused_ref[0])
    def _():
        o_ref[...] = jnp.zeros_like(o_ref)

    @pl.when(i < nused_ref[0])
    def _():
        if any(cast):
            def do_cast():
                for w_ref, wn_ref, wb_ref, s in zip(w_refs, wnext_refs, wb_refs, w_shift):
                    if wb_ref is None:
                        continue
                    if s:
                        tn = w_ref.shape[col_axis]
                        wide = jnp.concatenate([w_ref[...], wn_ref[...]], axis=col_axis)
                        wide = wide[s:s + tn, :] if w_transposed else wide[:, s:s + tn]
                        wb_ref[...] = wide.astype(BF16)
                    else:
                        wb_ref[...] = w_ref[...].astype(BF16)
            if nk == 1:
                changed = jnp.logical_or(i == 0, grp_ref[i] != grp_ref[jnp.maximum(i - 1, 0)])
                pl.when(changed)(do_cast)
            else:
                do_cast()
        parts = []
        for wi in range(n_w):
            w = wb_refs[wi][...] if cast[wi] else w_refs[wi][...]
            contract = (((1,), (1,)), ((), ())) if w_transposed else (((1,), (0,)), ((), ()))
            parts.append(lax.dot_general(a_refs[w_to_a[wi]][...], w, contract, preferred_element_type=F32))
        if nk == 1:
            o_ref[...] = epilogue(parts, e_refs).astype(o_ref.dtype)
        else:
            @pl.when(k == 0)
            def _():
                for acc, p in zip(acc_refs, parts):
                    acc[...] = p

            @pl.when(k > 0)
            def _():
                for acc, p in zip(acc_refs, parts):
                    acc[...] += p

            @pl.when(k == nk - 1)
            def _():
                o_ref[...] = epilogue([acc[...] for acc in acc_refs], e_refs).astype(o_ref.dtype)


def grouped_matmul(a_list, w_list, w_to_a, extras, epilogue, *, grp, nused, ncols, tm, tn, out_dtype,
                   w_col0=None, w_shift=None, w_transposed=False, extra_col0=(), tk=None):
    m, kdim = a_list[0].shape
    tk = kdim if tk is None else tk
    nk = kdim // tk
    w_col0 = (0,) * len(w_list) if w_col0 is None else tuple(w_col0)
    w_shift = (0,) * len(w_list) if w_shift is None else tuple(w_shift)
    assert m % tm == 0 and ncols % tn == 0 and kdim % tk == 0 and all(c % tn == 0 for c in w_col0)
    assert all(a.shape == (m, kdim) for a in a_list)
    assert all(w.shape[2 if w_transposed else 1] == kdim for w in w_list)
    cast = tuple(w.dtype != BF16 for w in w_list)
    assert all(c or not s for c, s in zip(cast, w_shift)) and all(0 <= s < LANES for s in w_shift)
    assert not w_transposed or all(s % 8 == 0 for s in w_shift)
    ecb = tuple(c // tn for c in extra_col0)
    lane_tiles = tn // LANES

    def row(i, n):
        return jnp.minimum(i, n[0] - 1)

    def w_block(width, col_index):
        if w_transposed:
            return pl.BlockSpec((None, width, tk), lambda j, i, k, g, n: (g[row(i, n)], col_index(j), k))
        return pl.BlockSpec((None, tk, width), lambda j, i, k, g, n: (g[row(i, n)], k, col_index(j)))

    def w_spec(cb):
        return w_block(tn, lambda j: cb + j)

    def w_next_spec(cb):
        return w_block(LANES, lambda j: (cb + j + 1) * lane_tiles)

    in_specs = [pl.BlockSpec((tm, tk), lambda j, i, k, g, n: (row(i, n), k)) for _ in a_list]
    in_specs += [w_spec(c // tn) for c in w_col0]
    in_specs += [w_next_spec(c // tn) for c, s in zip(w_col0, w_shift) if s]
    in_specs += [pl.BlockSpec((tm, tn), functools.partial(lambda j, i, k, g, n, eb: (row(i, n), eb + j), eb=eb))
                 for eb in ecb]
    scratch = [pltpu.VMEM((tn, tk) if w_transposed else (tk, tn), BF16) for c in cast if c]
    if nk > 1:
        scratch += [pltpu.VMEM((tm, tn), F32) for _ in w_list]
    kernel = functools.partial(_mm_kernel, n_a=len(a_list), w_to_a=tuple(w_to_a), cast=cast, w_shift=w_shift,
                               w_transposed=w_transposed, n_extra=len(extras), nk=nk, epilogue=epilogue)
    w_next = [w for w, s in zip(w_list, w_shift) if s]
    return pl.pallas_call(
        kernel,
        grid_spec=pltpu.PrefetchScalarGridSpec(
            num_scalar_prefetch=2,
            grid=(ncols // tn, m // tm, nk),
            in_specs=in_specs,
            out_specs=pl.BlockSpec((tm, tn), lambda j, i, k, g, n: (i, j)),
            scratch_shapes=scratch),
        out_shape=jax.ShapeDtypeStruct((m, ncols), out_dtype),
        compiler_params=_compiler_params(("arbitrary", "arbitrary", "arbitrary")),
    )(grp, nused, *a_list, *w_list, *w_next, *extras)


def _epi_plain(parts, e_refs):
    return parts[0]


def _epi_sigmoid(parts, e_refs):
    return _sigmoid(parts[0])


def _epi_gated_sum(parts, e_refs):
    return e_refs[0][...] * parts[0] + e_refs[1][...] * parts[1]


def _epi_residual(parts, e_refs):
    return e_refs[0][...] + parts[0]


def _epi_swiglu(parts, e_refs):
    return _silu(parts[0]) * parts[1]


def attention_tables(n_pad):
    slopes = np.exp2(-8.0 * (np.arange(HEADS, dtype=np.float64) + 1.0) / HEADS) * LOG2_E
    i = np.arange(ATTN_ROWS)[:, None]
    j = np.arange(ATTN_ROWS + Q_BLOCK)[None, :]
    visible = (i - N_META) // CHUNK >= (j - N_META) // CHUNK
    rel = (i - np.abs(i - j)).astype(np.float64)
    diag = np.where(visible[None], slopes[:, None, None] * rel[None], NEG_BIG)
    col = slopes[:, None, None] * np.arange(n_pad, dtype=np.float64)[None, None, :]
    return jnp.asarray(diag, F32), jnp.asarray(col, F32)


def _attn_kernel(q_ref, k_ref, v_ref, lam_ref, dn_ref, diag_ref, col_ref, o_ref, kt_ref, *, lambda_init):
    n_pad = q_ref.shape[0]
    lp = lam_ref[...]
    s1 = jnp.sum(lp[0:1] * lp[1:2], axis=-1, keepdims=True)
    s2 = jnp.sum(lp[2:3] * lp[3:4], axis=-1, keepdims=True)
    lam = jnp.exp(s1) - jnp.exp(s2) + lambda_init
    for mp in range(2):
        kt_ref[mp] = k_ref[:, mp * QK_DIM:(mp + 1) * QK_DIM].T
    scale2 = QK_DIM ** -0.5 * LOG2_E
    gain = dn_ref[...] * (1.0 - lambda_init)
    for q0 in range(0, n_pad, ATTN_ROWS):
        qb = q0 // ATTN_ROWS
        n_rows = min(ATTN_ROWS, n_pad - q0)
        width = min(n_rows + Q_BLOCK, n_pad - q0)
        rows = slice(q0, q0 + n_rows)
        diag_bias = diag_ref[:n_rows, :width]
        past_bias = col_ref[:, :q0] - col_ref[:, q0:q0 + 1] if qb else None
        heads = []
        for mp in range(2):
            q = q_ref[rows, mp * QK_DIM:(mp + 1) * QK_DIM]
            s_d = jnp.dot(q, kt_ref[mp, :, q0:q0 + width], preferred_element_type=F32) * scale2 + diag_bias
            mx = jnp.max(s_d, axis=-1, keepdims=True)
            if qb:
                s_p = jnp.dot(q, kt_ref[mp, :, :q0], preferred_element_type=F32) * scale2 + past_bias
                mx = jnp.maximum(mx, jnp.max(s_p, axis=-1, keepdims=True))
            e_d = jnp.exp2(s_d - mx)
            den = jnp.sum(e_d, axis=-1, keepdims=True)
            pv = jnp.dot(e_d.astype(BF16), v_ref[q0:q0 + width, :], preferred_element_type=F32)
            if qb:
                e_p = jnp.exp2(s_p - mx)
                den = den + jnp.sum(e_p, axis=-1, keepdims=True)
                pv = pv + jnp.dot(e_p.astype(BF16), v_ref[:q0, :], preferred_element_type=F32)
            heads.append(pv / den)
        o = heads[0] - lam * heads[1]
        ms = jnp.mean(o * o, axis=-1, keepdims=True)
        o_ref[rows, :] = (o * lax.rsqrt(ms + SUBLN_EPS) * gain).astype(o_ref.dtype)


def diff_attention(qkv, lam_stack, diff_norm_stack, layer, diag, col, *, bsz, n_pad):
    m = qkv.shape[0]
    kernel = functools.partial(_attn_kernel, lambda_init=0.8 - 0.6 * math.exp(-0.3 * layer))
    return pl.pallas_call(
        kernel,
        grid=(bsz, HEADS),
        in_specs=[
            pl.BlockSpec((n_pad, V_DIM), lambda b, h: (b, h)),
            pl.BlockSpec((n_pad, V_DIM), lambda b, h: (b, HEADS + h)),
            pl.BlockSpec((n_pad, V_DIM), lambda b, h: (b, 2 * HEADS + h)),
            _layer_spec(4, QK_DIM, layer),
            _layer_spec(1, V_DIM, layer),
            pl.BlockSpec((None, ATTN_ROWS, ATTN_ROWS + Q_BLOCK), lambda b, h: (h, 0, 0)),
            pl.BlockSpec((None, 1, n_pad), lambda b, h: (h, 0, 0)),
        ],
        out_specs=pl.BlockSpec((n_pad, V_DIM), lambda b, h: (b, h)),
        out_shape=jax.ShapeDtypeStruct((m, ATTN_WIDTH), BF16),
        scratch_shapes=[pltpu.VMEM((2, QK_DIM, n_pad), BF16)],
        compiler_params=_compiler_params(("parallel", "parallel")),
    )(qkv, qkv, qkv, lam_stack, diff_norm_stack, diag, col)


def _split3_bf16(x):
    hi = x.astype(BF16)
    r1 = x - hi.astype(F32)
    mid = r1.astype(BF16)
    lo = (r1 - mid.astype(F32)).astype(BF16)
    return hi, mid, lo


def _ssd_kernel(zx_ref, dtr_ref, cw_ref, cb_ref, dtb_ref, alog_ref, dsk_ref, nw_ref, ex_ref, o_ref,
                state_ref, xbuf_ref, y_ref):
    c = pl.program_id(1)
    L = SSD_BLOCK
    halo = 8

    @pl.when(c == 0)
    def _():
        state_ref[...] = jnp.zeros_like(state_ref)
        xbuf_ref[0:halo, :] = jnp.zeros((halo, XBC_WIDTH), F32)

    @pl.when(c > 0)
    def _():
        xbuf_ref[0:halo, :] = xbuf_ref[L:L + halo, :]

    u = zx_ref[:, SSM_WIDTH:SSM_WIDTH + XBC_WIDTH]
    xbuf_ref[halo:halo + L, :] = u
    acc = cb_ref[...] + cw_ref[SSM_CONV - 1:SSM_CONV, :] * u
    for s in range(1, SSM_CONV):
        acc = acc + cw_ref[SSM_CONV - 1 - s:SSM_CONV - s, :] * xbuf_ref[halo - s:halo - s + L, :]
    xbc = _silu(acc)
    xs = xbc[:, :SSM_WIDTH]
    bm = xbc[:, SSM_WIDTH:SSM_WIDTH + SSM_GROUPS * SSM_STATE].astype(BF16)
    cm = xbc[:, SSM_WIDTH + SSM_GROUPS * SSM_STATE:].astype(BF16)

    pre = dtr_ref[...] + dtb_ref[...]
    dt = jnp.maximum(pre, 0.0) + jnp.log(1.0 + jnp.exp(-jnp.abs(pre)))
    adt = dt * (-jnp.exp(alog_ref[...]))
    rows = lax.broadcasted_iota(jnp.int32, (L, LANES), 0)
    cum = adt
    shift = 1
    while shift < L:
        cum = cum + jnp.where(rows >= shift, pltpu.roll(cum, shift, 0), 0.0)
        shift *= 2
    cum_last = cum[L - 1:L, :]
    ecum = jnp.exp(cum)
    edec = jnp.exp(cum_last - cum)
    elast = jnp.broadcast_to(jnp.exp(cum_last), (8, LANES))
    stacked = jnp.concatenate([dt, ecum, edec, elast], axis=0)
    ex = ex_ref[...]
    wide = sum(jnp.dot(p, ex, preferred_element_type=F32) for p in _split3_bf16(stacked))
    dt_w, ecum_w, edec_w, elast_w = wide[0:L], wide[L:2 * L], wide[2 * L:3 * L], wide[3 * L:3 * L + 1]

    xd = xs * dt_w
    xd_b = xd.astype(BF16)
    xdec_b = (xd * edec_w).astype(BF16)
    cum_t = cum.T
    tril = lax.broadcasted_iota(jnp.int32, (L, L), 0) >= lax.broadcasted_iota(jnp.int32, (L, L), 1)
    heads_per_group = SSM_HEADS // SSM_GROUPS
    for g in range(SSM_GROUPS):
        gs = slice(g * SSM_GROUP_WIDTH, (g + 1) * SSM_GROUP_WIDTH)
        bg = bm[:, g * SSM_STATE:(g + 1) * SSM_STATE]
        cg = cm[:, g * SSM_STATE:(g + 1) * SSM_STATE]
        cb = lax.dot_general(cg, bg, (((1,), (1,)), ((), ())), preferred_element_type=F32)
        st = state_ref[:, gs]
        y_off = jnp.dot(cg, st.astype(BF16), preferred_element_type=F32) * ecum_w[:, gs]
        state_ref[:, gs] = elast_w[:, gs] * st + lax.dot_general(
            bg, xdec_b[:, gs], (((0,), (0,)), ((), ())), preferred_element_type=F32)
        y_ref[:, gs] = y_off
        for r in range(heads_per_group):
            hd = g * heads_per_group + r
            hs = slice(hd * SSM_HEAD_DIM, (hd + 1) * SSM_HEAD_DIM)
            seg = cum[:, hd:hd + 1] - cum_t[hd:hd + 1, :]
            lmat = jnp.exp(jnp.where(tril, seg, -jnp.inf))
            y_ref[:, hs] += jnp.dot((cb * lmat).astype(BF16), xd_b[:, hs], preferred_element_type=F32)

    y = y_ref[...] + dsk_ref[...] * xs
    gated = y * _silu(zx_ref[:, 0:SSM_WIDTH])
    for g in range(SSM_GROUPS):
        gs = slice(g * SSM_GROUP_WIDTH, (g + 1) * SSM_GROUP_WIDTH)
        gg = gated[:, gs]
        ms = jnp.mean(gg * gg, axis=-1, keepdims=True)
        o_ref[:, gs] = (gg * lax.rsqrt(ms + NORM_EPS) * nw_ref[:, gs]).astype(o_ref.dtype)


def ssd_param_stacks(conv_b, dt_bias, a_log, d_skip, ssm_norm):
    n_layers = conv_b.shape[0]

    def pad_heads(v):
        return jnp.pad(v.astype(F32), ((0, 0), (0, LANES - SSM_HEADS))).reshape(n_layers, 1, LANES)

    return (conv_b.reshape(n_layers, 1, XBC_WIDTH), pad_heads(dt_bias), pad_heads(a_log),
            jnp.repeat(d_skip.astype(F32), SSM_HEAD_DIM, axis=1).reshape(n_layers, 1, SSM_WIDTH),
            ssm_norm.reshape(n_layers, 1, SSM_WIDTH))


def ssd_mixer(zx, dtr, conv_w, stacks, layer, expand, *, bsz, n_pad):
    m = zx.shape[0]
    nc = n_pad // SSD_BLOCK
    L = SSD_BLOCK
    return pl.pallas_call(
        _ssd_kernel,
        grid=(bsz, nc),
        in_specs=[
            pl.BlockSpec((L, ZX_COLS), lambda b, c: (b * nc + c, 0)),
            pl.BlockSpec((L, LANES), lambda b, c: (b * nc + c, 0)),
            _layer_spec(SSM_CONV, XBC_WIDTH, layer), _layer_spec(1, XBC_WIDTH, layer), _layer_spec(1, LANES, layer),
            _layer_spec(1, LANES, layer), _layer_spec(1, SSM_WIDTH, layer), _layer_spec(1, SSM_WIDTH, layer),
            pl.BlockSpec((LANES, SSM_WIDTH), lambda b, c: (0, 0)),
        ],
        out_specs=pl.BlockSpec((L, SSM_WIDTH), lambda b, c: (b * nc + c, 0)),
        out_shape=jax.ShapeDtypeStruct((m, SSM_WIDTH), BF16),
        scratch_shapes=[pltpu.VMEM((SSM_STATE, SSM_WIDTH), F32), pltpu.VMEM((L + 8, XBC_WIDTH), F32),
                        pltpu.VMEM((L, SSM_WIDTH), F32)],
        compiler_params=_compiler_params(("parallel", "arbitrary")),
    )(zx, dtr, conv_w, *stacks, expand)


def _dispatch_kernel(tok_ref, nused_ref, h_hbm, w_ref, o_ref, buf_ref, sem):
    i = pl.program_id(0)
    g = buf_ref.shape[1]
    nused = nused_ref[0]
    slot = i % 2

    def row_copy(src_row, dst_slot, r):
        return pltpu.make_async_copy(h_hbm.at[pl.ds(src_row, 1)], buf_ref.at[dst_slot, pl.ds(r, 1)],
                                     sem.at[dst_slot])

    def start_tile(tile, dst_slot):
        def body(r, carry):
            row_copy(tok_ref[tile * g + r], dst_slot, r).start()
            return carry
        lax.fori_loop(0, g, body, 0, unroll=8)

    def wait_tile(dst_slot):
        def body(r, carry):
            row_copy(0, dst_slot, r).wait()
            return carry
        lax.fori_loop(0, g, body, 0, unroll=8)

    @pl.when(jnp.logical_and(i == 0, nused > 0))
    def _():
        start_tile(0, 0)

    @pl.when(i + 1 < nused)
    def _():
        start_tile(i + 1, 1 - slot)

    @pl.when(i < nused)
    def _():
        wait_tile(slot)
        x = buf_ref[slot]
        ms = jnp.mean(x * x, axis=-1, keepdims=True)
        o_ref[...] = (x * lax.rsqrt(ms + NORM_EPS) * w_ref[...]).astype(o_ref.dtype)

    @pl.when(i >= nused)
    def _():
        o_ref[...] = jnp.zeros_like(o_ref)


def moe_dispatch(h, w_stack, layer, tok, nused, *, tg=MOE_ROW_TILE):
    d = h.shape[1]
    rows = tok.shape[0]
    return pl.pallas_call(
        _dispatch_kernel,
        grid_spec=pltpu.PrefetchScalarGridSpec(
            num_scalar_prefetch=2,
            grid=(rows // tg,),
            in_specs=[pl.BlockSpec(memory_space=pl.ANY), _layer_spec(1, d, layer)],
            out_specs=pl.BlockSpec((tg, d), lambda i, t, n: (i, 0)),
            scratch_shapes=[pltpu.VMEM((2, tg, d), F32), pltpu.SemaphoreType.DMA((2,))]),
        out_shape=jax.ShapeDtypeStruct((rows, d), BF16),
        compiler_params=_compiler_params(("arbitrary",)),
    )(tok, nused, h, w_stack)


def _combine_kernel(pos1_ref, pos2_ref, y_hbm, h_hbm, route_hbm, w_ref, o_ref,
                    y1_ref, y2_ref, h_ref, r_ref, sem_y, sem_h, *, n_pad, n_frames):
    i = pl.program_id(0)
    n_tiles = pl.num_programs(0)
    tr = h_ref.shape[1]
    tiles_per_batch = n_frames // tr
    slot = i % 2

    def block_copies(tile, dst_slot):
        b = tile // tiles_per_batch
        row0 = b * n_pad + N_META + (tile - b * tiles_per_batch) * tr
        return (pltpu.make_async_copy(h_hbm.at[pl.ds(row0, tr)], h_ref.at[dst_slot], sem_h.at[0, dst_slot]),
                pltpu.make_async_copy(route_hbm.at[pl.ds(row0, tr)], r_ref.at[dst_slot], sem_h.at[1, dst_slot]))

    def y_copies(row1, row2, dst_slot, r):
        return (pltpu.make_async_copy(y_hbm.at[pl.ds(row1, 1)], y1_ref.at[dst_slot, pl.ds(r, 1)], sem_y.at[dst_slot]),
                pltpu.make_async_copy(y_hbm.at[pl.ds(row2, 1)], y2_ref.at[dst_slot, pl.ds(r, 1)], sem_y.at[dst_slot]))

    def start_tile(tile, dst_slot):
        for cp in block_copies(tile, dst_slot):
            cp.start()

        def body(r, carry):
            t = tile * tr + r
            for cp in y_copies(pos1_ref[t], pos2_ref[t], dst_slot, r):
                cp.start()
            return carry
        lax.fori_loop(0, tr, body, 0, unroll=8)

    def wait_tile(tile, dst_slot):
        for cp in block_copies(tile, dst_slot):
            cp.wait()

        def body(r, carry):
            for cp in y_copies(0, 0, dst_slot, r):
                cp.wait()
            return carry
        lax.fori_loop(0, tr, body, 0, unroll=8)

    @pl.when(i == 0)
    def _():
        start_tile(0, 0)

    @pl.when(i + 1 < n_tiles)
    def _():
        start_tile(i + 1, 1 - slot)

    wait_tile(i, slot)
    route = r_ref[slot]
    lane = lax.broadcasted_iota(jnp.int32, route.shape, 1).astype(F32)
    in_experts = lane < N_EXPERTS
    w1 = jnp.sum(jnp.where(jnp.logical_and(in_experts, lane == route[:, N_EXPERTS:N_EXPERTS + 1]), route, 0.0),
                 axis=-1, keepdims=True)
    w2 = jnp.sum(jnp.where(jnp.logical_and(in_experts, lane == route[:, N_EXPERTS + 1:N_EXPERTS + 2]), route, 0.0),
                 axis=-1, keepdims=True)
    x = h_ref[slot] + w1 * y1_ref[slot] + w2 * y2_ref[slot]
    ms = jnp.mean(x * x, axis=-1, keepdims=True)
    o_ref[...] = x * lax.rsqrt(ms + NORM_EPS) * w_ref[...]


def moe_combine(ys, h, route, final_norm, pos1, pos2, *, bsz, n_pad, n_frames, tr=MOE_ROW_TILE):
    d = h.shape[1]
    kernel = functools.partial(_combine_kernel, n_pad=n_pad, n_frames=n_frames)
    any_spec = pl.BlockSpec(memory_space=pl.ANY)
    return pl.pallas_call(
        kernel,
        grid_spec=pltpu.PrefetchScalarGridSpec(
            num_scalar_prefetch=2,
            grid=(bsz * n_frames // tr,),
            in_specs=[any_spec, any_spec, any_spec, pl.BlockSpec((1, d), lambda i, p1, p2: (0, 0))],
            out_specs=pl.BlockSpec((tr, d), lambda i, p1, p2: (i, 0)),
            scratch_shapes=[pltpu.VMEM((2, tr, d), F32), pltpu.VMEM((2, tr, d), F32), pltpu.VMEM((2, tr, d), F32),
                            pltpu.VMEM((2, tr, LANES), F32), pltpu.SemaphoreType.DMA((2,)),
                            pltpu.SemaphoreType.DMA((2, 2))]),
        out_shape=jax.ShapeDtypeStruct((bsz * n_frames, d), F32),
        compiler_params=_compiler_params(("arbitrary",)),
    )(pos1, pos2, ys, h, route, final_norm.reshape(1, d))


def _moe_plan(route, *, bsz, n_pad, n_frames, tm):
    n_tok = bsz * n_frames
    cap = 2 * n_tok + N_EXPERTS * tm
    frames = route.reshape(bsz, n_pad, LANES)[:, N_META:N_META + n_frames].reshape(n_tok, LANES)
    idx = frames[:, N_EXPERTS:N_EXPERTS + 2].astype(jnp.int32)
    sel = (idx[:, :, None] == jnp.arange(N_EXPERTS)[None, None, :]).any(axis=1)
    counts = sel.sum(axis=0)
    padded = ((counts + tm - 1) // tm) * tm
    ends = jnp.cumsum(padded)
    starts = ends - padded
    rank = jnp.cumsum(sel, axis=0) - 1
    pos = starts[None, :] + rank
    pos12 = jnp.take_along_axis(pos, idx, axis=1).astype(jnp.int32)
    tok_rows = (jnp.arange(n_tok) // n_frames) * n_pad + N_META + jnp.arange(n_tok) % n_frames
    tok = jnp.zeros((cap,), jnp.int32).at[pos12.reshape(-1)].set(
        jnp.repeat(tok_rows.astype(jnp.int32), 2), mode="drop")
    tile_start = jnp.arange(cap // tm) * tm
    tile_grp = jnp.minimum((tile_start[:, None] >= ends[None, :]).sum(axis=1), N_EXPERTS - 1).astype(jnp.int32)
    rows_used = ends[-1].astype(jnp.int32).reshape(1)
    return tok, tile_grp, rows_used, pos12[:, 0], pos12[:, 1]


def _dense_plan(m, tm, group):
    return jnp.asarray(np.full((m // tm,), group, np.int32)), jnp.asarray(np.full((1,), m // tm, np.int32))


def mixer_tables(n_pad):
    diag, col = attention_tables(n_pad)
    expand = np.arange(LANES)[:, None] == (np.arange(SSM_WIDTH) // SSM_HEAD_DIM)[None, :]
    return {"attn_diag": diag, "attn_col": col, "expand": jnp.asarray(expand, BF16)}


def mixer_param_stacks(mix_norm, conv_b, dt_bias, a_log, d_skip, ssm_norm, lambda_q1, lambda_k1, lambda_q2, lambda_k2,
                       diff_norm):
    n_layers, d = mix_norm.shape
    return {
        "mix_norm": mix_norm.reshape(n_layers, 1, d),
        "ssd": ssd_param_stacks(conv_b, dt_bias, a_log, d_skip, ssm_norm),
        "lam": jnp.stack([lambda_q1, lambda_k1, lambda_q2, lambda_k2], axis=1).astype(F32),
        "diff_norm": diff_norm.reshape(n_layers, 1, V_DIM),
    }


def mixer_layer(h, layer, p, *, bsz, n_pad, tm, hn=None):
    m, d = h.shape
    grp, nused = _dense_plan(m, tm, layer)
    mm = functools.partial(grouped_matmul, grp=grp, nused=nused, tm=tm)
    if hn is None:
        hn = rmsnorm(h, p["mix_norm"], layer)
    mm_in = functools.partial(mm, [hn], [p["mix_w_in_t"]], [0], [], w_transposed=True)
    qkv = mm_in(_epi_plain, ncols=QKV_COLS, tn=512, out_dtype=BF16)
    zx = mm_in(_epi_plain, ncols=ZX_COLS, tn=512, out_dtype=F32, w_col0=(ZX_COL0,))
    dtr = mm_in(_epi_plain, ncols=LANES, tn=LANES, out_dtype=F32, w_col0=(DT_COL0,))
    gates = mm_in(_epi_sigmoid, ncols=2 * d, tn=512, out_dtype=BF16, w_col0=(DT_COL0,),
                  w_shift=(GATE_COL0 - DT_COL0,))
    o_attn = diff_attention(qkv, p["lam"], p["diff_norm"], layer, p["attn_diag"], p["attn_col"], bsz=bsz,
                            n_pad=n_pad)
    o_ssm = ssd_mixer(zx, dtr, p["conv_w"], p["ssd"], layer, p["expand"], bsz=bsz, n_pad=n_pad)
    merged = mm([o_attn, o_ssm], [p["w_proj_attn"], p["w_proj_ssm"]], [0, 1], [gates, gates], _epi_gated_sum,
                ncols=d, tn=512, out_dtype=BF16, extra_col0=(0, d))
    return mm([merged], [p["w_out"]], [0], [h], _epi_residual, ncols=d, tn=512, out_dtype=F32, extra_col0=(0,))


def dense_ffn(h, norm_stack, layer, w_gate, w_up, w_down, group, *, tm):
    m, d = h.shape
    d_ff = w_gate.shape[2]
    grp, nused = _dense_plan(m, tm, group)
    mm = functools.partial(grouped_matmul, grp=grp, nused=nused, tm=tm)
    hn = rmsnorm(h, norm_stack, layer)
    u = mm([hn], [w_gate, w_up], [0, 0], [], _epi_swiglu, ncols=d_ff, tn=256, out_dtype=BF16)
    tm_down = tm // 2
    grp, nused = _dense_plan(m, tm_down, group)
    return grouped_matmul([u], [w_down.astype(BF16)], [0], [h], _epi_residual, grp=grp, nused=nused, ncols=d,
                          tm=tm_down, tn=512, out_dtype=F32, extra_col0=(0,))


def moe_ffn_final(h, norm_stack, layer, router_w, w_gate, w_up, w_down, final_norm, group, *, bsz, n_pad, n_frames):
    d = h.shape[1]
    d_fe = w_gate.shape[3]
    r_pad = jnp.pad(router_w.astype(F32), ((0, 0), (0, LANES - N_EXPERTS)))
    route = route_tokens(h, norm_stack, layer, r_pad)
    tok, tile_grp, rows_used, pos1, pos2 = _moe_plan(route, bsz=bsz, n_pad=n_pad, n_frames=n_frames,
                                                      tm=MOE_MM_ROW_TILE)
    xs = moe_dispatch(h, norm_stack, layer, tok, rows_used // MOE_ROW_TILE)
    mm = functools.partial(grouped_matmul, grp=tile_grp + group * N_EXPERTS, nused=rows_used // MOE_MM_ROW_TILE,
                           tm=MOE_MM_ROW_TILE)
    u = mm([xs], [w_gate.reshape(-1, d, d_fe), w_up.reshape(-1, d, d_fe)], [0, 0], [], _epi_swiglu,
           ncols=d_fe, tn=512, out_dtype=BF16)
    ys = mm([u], [w_down.reshape(-1, d_fe, d)], [0], [], _epi_plain, ncols=d, tn=1024, out_dtype=F32)
    return moe_combine(ys, h, route, final_norm, pos1, pos2, bsz=bsz, n_pad=n_pad, n_frames=n_frames)


def kernel(x, meta_tokens, mix_norm, mix_w_in, conv_w, conv_b, dt_bias, a_log, d_skip, ssm_norm, lambda_q1, lambda_k1, lambda_q2, lambda_k2, diff_norm, w_proj_attn, w_proj_ssm, w_out, ffn_norm, ffn_w_gate, ffn_w_up, ffn_w_down, router, moe_w_gate, moe_w_up, moe_w_down, final_norm):
    bsz, n_frames, d = x.shape
    assert mix_norm.shape[0] == 2 and d == D_MODEL, "layer 0: dense FFN; layer 1: MoE FFN, then the final norm"
    n = N_META + n_frames
    n_pad = -(-n // Q_BLOCK) * Q_BLOCK
    m = bsz * n_pad
    tm = n_pad // 2
    assert tm % 16 == 0 and n_pad % SSD_BLOCK == 0 and n_frames % MOE_ROW_TILE == 0

    p = dict(mix_w_in_t=jnp.swapaxes(mix_w_in, 1, 2), conv_w=conv_w, w_proj_attn=w_proj_attn, w_proj_ssm=w_proj_ssm,
             w_out=w_out)
    p.update(mixer_param_stacks(mix_norm, conv_b, dt_bias, a_log, d_skip, ssm_norm, lambda_q1, lambda_k1, lambda_q2,
                                lambda_k2, diff_norm))
    p.update(mixer_tables(n_pad))
    ffn_norm_stack = ffn_norm.reshape(ffn_norm.shape[0], 1, d)
    h, hn = embed_norm(x, meta_tokens, p["mix_norm"], 0)
    assert h.shape == (m, d)
    h = mixer_layer(h, 0, p, bsz=bsz, n_pad=n_pad, tm=tm, hn=hn)
    h = dense_ffn(h, ffn_norm_stack, 0, ffn_w_gate, ffn_w_up, ffn_w_down, 0, tm=tm)
    h = mixer_layer(h, 1, p, bsz=bsz, n_pad=n_pad, tm=tm)
    out = moe_ffn_final(h, ffn_norm_stack, 1, router[0], moe_w_gate, moe_w_up, moe_w_down, final_norm, 0,
                        bsz=bsz, n_pad=n_pad, n_frames=n_frames)
    return out.reshape(bsz, n_frames, d)
```

```python
import functools
import math

import jax
import jax.numpy as jnp
import numpy as np
from jax import lax
from jax.experimental import pallas as pl
from jax.experimental.pallas import tpu as pltpu

F32 = jnp.float32
BF16 = jnp.bfloat16

LANES = 128
VMEM_LIMIT_BYTES_V7X = 56 * 2**20

D_MODEL = 4096
N_META = 16
CHUNK = 64
Q_BLOCK = 128
ATTN_ROWS = 2 * Q_BLOCK
HEADS = 8
QK_DIM = 128
V_DIM = 256
ATTN_WIDTH = HEADS * V_DIM
SUBLN_EPS = 1e-5
SSM_WIDTH = 2048
SSM_HEAD_DIM = 64
SSM_HEADS = 32
SSM_GROUPS = 4
SSM_STATE = 128
SSM_CONV = 4
SSM_GROUP_WIDTH = SSM_WIDTH // SSM_GROUPS
XBC_WIDTH = SSM_WIDTH + 2 * SSM_GROUPS * SSM_STATE
QKV_COLS = 3 * ATTN_WIDTH
ZX_COL0 = QKV_COLS
ZX_COLS = SSM_WIDTH + XBC_WIDTH
DT_COL0 = ZX_COL0 + ZX_COLS
GATE_COL0 = DT_COL0 + SSM_HEADS
N_EXPERTS = 8
NORM_EPS = 1e-6
NEG_BIG = -1e30
LOG2_E = 1.4426950408889634
SSD_BLOCK = 128

A_COPY_STREAMS = 2
MOE_ROW_TILE = 256
MOE_MM_ROW_TILE = 512
NORM_ROW_TILE = 272


def _silu(x):
    return x * (1.0 / (1.0 + jnp.exp(-x)))


def _sigmoid(x):
    return 1.0 / (1.0 + jnp.exp(-x))


def _compiler_params(semantics):
    return pltpu.CompilerParams(dimension_semantics=semantics, vmem_limit_bytes=VMEM_LIMIT_BYTES_V7X)


def _rmsnorm_kernel(x_ref, w_ref, o_ref, *, eps):
    x = x_ref[...]
    ms = jnp.mean(x * x, axis=-1, keepdims=True)
    o_ref[...] = (x * lax.rsqrt(ms + eps) * w_ref[...]).astype(o_ref.dtype)


def _norm_row_tile(m):
    return NORM_ROW_TILE if m % NORM_ROW_TILE == 0 else Q_BLOCK


def _layer_spec(rows, width, layer):
    return pl.BlockSpec((None, rows, width), lambda *_: (layer, 0, 0))


def rmsnorm(x, w_stack, layer, *, eps=NORM_EPS, out_dtype=BF16):
    m, d = x.shape
    tr = _norm_row_tile(m)
    return pl.pallas_call(
        functools.partial(_rmsnorm_kernel, eps=eps),
        grid=(m // tr,),
        in_specs=[pl.BlockSpec((tr, d), lambda i: (i, 0)), _layer_spec(1, d, layer)],
        out_specs=pl.BlockSpec((tr, d), lambda i: (i, 0)),
        out_shape=jax.ShapeDtypeStruct((m, d), out_dtype),
        compiler_params=_compiler_params(("parallel",)),
    )(x, w_stack)


def _embed_norm_kernel(x_ref, meta_ref, w_ref, h_ref, hn_ref, *, eps):
    c = pl.program_id(1)
    last = pl.num_programs(1) - 1
    tail = Q_BLOCK - N_META

    def emit(rows):
        h_ref[...] = rows
        ms = jnp.mean(rows * rows, axis=-1, keepdims=True)
        hn_ref[...] = (rows * lax.rsqrt(ms + eps) * w_ref[...]).astype(hn_ref.dtype)

    @pl.when(c == 0)
    def _():
        emit(jnp.concatenate([meta_ref[...], x_ref[0, 0:tail, :]], axis=0))

    @pl.when(jnp.logical_and(c > 0, c < last))
    def _():
        emit(x_ref[0])

    @pl.when(c == last)
    def _():
        emit(jnp.concatenate([x_ref[0, tail:Q_BLOCK, :], jnp.zeros((tail, x_ref.shape[2]), F32)], axis=0))


def embed_norm(x, meta_tokens, w_stack, layer):
    bsz, n_frames, d = x.shape
    assert n_frames % Q_BLOCK == 0 and meta_tokens.shape == (N_META, d)
    nc = n_frames // Q_BLOCK + 1
    m = bsz * nc * Q_BLOCK

    def x_index(b, c):
        start = jnp.clip(c * Q_BLOCK - N_META, 0, n_frames - Q_BLOCK)
        return b, pl.multiple_of(start, N_META), 0

    row_spec = pl.BlockSpec((Q_BLOCK, d), lambda b, c: (b * nc + c, 0))
    return pl.pallas_call(
        functools.partial(_embed_norm_kernel, eps=NORM_EPS),
        grid=(bsz, nc),
        in_specs=[pl.BlockSpec((pl.Element(1), pl.Element(Q_BLOCK), pl.Element(d)), x_index),
                  pl.BlockSpec((N_META, d), lambda b, c: (0, 0)), _layer_spec(1, d, layer)],
        out_specs=[row_spec, row_spec],
        out_shape=[jax.ShapeDtypeStruct((m, d), F32), jax.ShapeDtypeStruct((m, d), BF16)],
        compiler_params=_compiler_params(("parallel", "arbitrary")),
    )(x, meta_tokens.astype(F32), w_stack)


def _split_bf16(x):
    hi = x.astype(BF16)
    lo = (x - hi.astype(F32)).astype(BF16)
    return hi, lo


def _router_kernel(x_ref, w_ref, r_ref, o_ref, *, eps):
    x = x_ref[...]
    ms = jnp.mean(x * x, axis=-1, keepdims=True)
    hn = x * lax.rsqrt(ms + eps) * w_ref[...]
    h_hi, h_lo = _split_bf16(hn)
    r_hi, r_lo = _split_bf16(r_ref[...])
    logits = (jnp.dot(h_hi, r_hi, preferred_element_type=F32)
              + jnp.dot(h_hi, r_lo, preferred_element_type=F32)
              + jnp.dot(h_lo, r_hi, preferred_element_type=F32))
    lane = lax.broadcasted_iota(jnp.int32, logits.shape, 1).astype(F32)
    lg = jnp.where(lane < N_EXPERTS, logits, -jnp.inf)
    m1 = jnp.max(lg, axis=-1, keepdims=True)
    i1 = jnp.min(jnp.where(lg == m1, lane, float(LANES)), axis=-1, keepdims=True)
    lg2 = jnp.where(lane == i1, -jnp.inf, lg)
    m2 = jnp.max(lg2, axis=-1, keepdims=True)
    i2 = jnp.min(jnp.where(lg2 == m2, lane, float(LANES)), axis=-1, keepdims=True)
    e2 = jnp.exp(m2 - m1)
    den = 1.0 + e2
    out = jnp.where(lane == i1, 1.0 / den, 0.0) + jnp.where(lane == i2, e2 / den, 0.0)
    out = jnp.where(lane == N_EXPERTS, i1, out)
    out = jnp.where(lane == N_EXPERTS + 1, i2, out)
    o_ref[...] = out


def route_tokens(x, w_stack, layer, r_pad):
    m, d = x.shape
    tr = _norm_row_tile(m)
    return pl.pallas_call(
        functools.partial(_router_kernel, eps=NORM_EPS),
        grid=(m // tr,),
        in_specs=[pl.BlockSpec((tr, d), lambda i: (i, 0)), _layer_spec(1, d, layer),
                  pl.BlockSpec((d, LANES), lambda i: (0, 0))],
        out_specs=pl.BlockSpec((tr, LANES), lambda i: (i, 0)),
        out_shape=jax.ShapeDtypeStruct((m, LANES), F32),
        compiler_params=_compiler_params(("parallel",)),
    )(x, w_stack, r_pad)


def _mm_kernel(grp_ref, nused_ref, *refs, n_a, a_split, w_to_a, cast, w_shift, w_transposed, n_extra, nk, epilogue):
    col_axis = 0 if w_transposed else 1
    n_w = len(w_to_a)
    a_refs = refs[:n_a * a_split]
    w_refs = refs[n_a * a_split:n_a * a_split + n_w]
    refs = list(refs[n_a * a_split + n_w:])
    wnext_refs = [refs.pop(0) if s else None for s in w_shift]
    e_refs = [refs.pop(0) for _ in range(n_extra)]
    o_ref = refs.pop(0)
    scratch = refs
    wb_refs = [scratch.pop(0) if c else None for c in cast]
    acc_refs = [scratch.pop(0) for _ in range(n_w)] if nk > 1 else []
    i = pl.program_id(1)
    k = pl.program_id(2)

    @pl.when(i >= nused_ref[0])
    def _():
        o_ref[...] = jnp.zeros_like(o_ref)

    @pl.when(i < nused_ref[0])
    def _():
        if any(cast):
            def do_cast():
                for w_ref, wn_ref, wb_ref, s in zip(w_refs, wnext_refs, wb_refs, w_shift):
                    if wb_ref is None:
                        continue
                    if s:
                        tn = w_ref.shape[col_axis]
                        wide = jnp.concatenate([w_ref[...], wn_ref[...]], axis=col_axis)
                        wide = wide[s:s + tn, :] if w_transposed else wide[:, s:s + tn]
                        wb_ref[...] = wide.astype(BF16)
                    else:
                        wb_ref[...] = w_ref[...].astype(BF16)
            if nk == 1:
                changed = jnp.logical_or(i == 0, grp_ref[i] != grp_ref[jnp.maximum(i - 1, 0)])
                pl.when(changed)(do_cast)
            else:
                do_cast()
        parts = []
        contract = (((1,), (1,)), ((), ())) if w_transposed else (((1,), (0,)), ((), ()))
        for wi in range(n_w):
            w_src = wb_refs[wi] if cast[wi] else w_refs[wi]
            part = None
            for s in range(a_split):
                a_ref = a_refs[w_to_a[wi] * a_split + s]
                ks = slice(s * a_ref.shape[1], (s + 1) * a_ref.shape[1])
                w = w_src[:, ks] if w_transposed else w_src[ks, :]
                prod = lax.dot_general(a_ref[...], w, contract, preferred_element_type=F32)
                part = prod if part is None else part + prod
            parts.append(part)
        if nk == 1:
            o_ref[...] = epilogue(parts, e_refs).astype(o_ref.dtype)
        else:
            @pl.when(k == 0)
            def _():
                for acc, p in zip(acc_refs, parts):
                    acc[...] = p

            @pl.when(k > 0)
            def _():
                for acc, p in zip(acc_refs, parts):
                    acc[...] += p

            @pl.when(k == nk - 1)
            def _():
                o_ref[...] = epilogue([acc[...] for acc in acc_refs], e_refs).astype(o_ref.dtype)


def grouped_matmul(a_list, w_list, w_to_a, extras, epilogue, *, grp, nused, ncols, tm, tn, out_dtype,
                   w_col0=None, w_shift=None, w_transposed=False, extra_col0=(), tk=None):
    m, kdim = a_list[0].shape
    tk = kdim if tk is None else tk
    nk = kdim // tk
    w_col0 = (0,) * len(w_list) if w_col0 is None else tuple(w_col0)
    w_shift = (0,) * len(w_list) if w_shift is None else tuple(w_shift)
    assert m % tm == 0 and ncols % tn == 0 and kdim % tk == 0 and all(c % tn == 0 for c in w_col0)
    assert all(a.shape == (m, kdim) for a in a_list)
    assert all(w.shape[2 if w_transposed else 1] == kdim for w in w_list)
    cast = tuple(w.dtype != BF16 for w in w_list)
    assert all(c or not s for c, s in zip(cast, w_shift)) and all(0 <= s < LANES for s in w_shift)
    assert not w_transposed or all(s % 8 == 0 for s in w_shift)
    ecb = tuple(c // tn for c in extra_col0)
    lane_tiles = tn // LANES

    def row(i, n):
        return jnp.minimum(i, n[0] - 1)

    def w_block(width, col_index):
        if w_transposed:
            return pl.BlockSpec((None, width, tk), lambda j, i, k, g, n: (g[row(i, n)], col_index(j), k))
        return pl.BlockSpec((None, tk, width), lambda j, i, k, g, n: (g[row(i, n)], k, col_index(j)))

    def w_spec(cb):
        return w_block(tn, lambda j: cb + j)

    def w_next_spec(cb):
        return w_block(LANES, lambda j: (cb + j + 1) * lane_tiles)

    a_split = A_COPY_STREAMS if tk % (A_COPY_STREAMS * LANES) == 0 else 1
    tks = tk // a_split
    in_specs = [pl.BlockSpec((tm, tks), functools.partial(lambda j, i, k, g, n, s: (row(i, n), k * a_split + s), s=s))
                for _ in a_list for s in range(a_split)]
    in_specs += [w_spec(c // tn) for c in w_col0]
    in_specs += [w_next_spec(c // tn) for c, s in zip(w_col0, w_shift) if s]
    in_specs += [pl.BlockSpec((tm, tn), functools.partial(lambda j, i, k, g, n, eb: (row(i, n), eb + j), eb=eb))
                 for eb in ecb]
    scratch = [pltpu.VMEM((tn, tk) if w_transposed else (tk, tn), BF16) for c in cast if c]
    if nk > 1:
        scratch += [pltpu.VMEM((tm, tn), F32) for _ in w_list]
    kernel = functools.partial(_mm_kernel, n_a=len(a_list), a_split=a_split, w_to_a=tuple(w_to_a), cast=cast,
                               w_shift=w_shift, w_transposed=w_transposed, n_extra=len(extras), nk=nk,
                               epilogue=epilogue)
    w_next = [w for w, s in zip(w_list, w_shift) if s]
    a_inputs = [a for a in a_list for _ in range(a_split)]
    return pl.pallas_call(
        kernel,
        grid_spec=pltpu.PrefetchScalarGridSpec(
            num_scalar_prefetch=2,
            grid=(ncols // tn, m // tm, nk),
            in_specs=in_specs,
            out_specs=pl.BlockSpec((tm, tn), lambda j, i, k, g, n: (i, j)),
            scratch_shapes=scratch),
        out_shape=jax.ShapeDtypeStruct((m, ncols), out_dtype),
        compiler_params=_compiler_params(("arbitrary", "arbitrary", "arbitrary")),
    )(grp, nused, *a_inputs, *w_list, *w_next, *extras)


def _epi_plain(parts, e_refs):
    return parts[0]


def _epi_sigmoid(parts, e_refs):
    return _sigmoid(parts[0])


def _epi_gated_sum(parts, e_refs):
    return e_refs[0][...] * parts[0] + e_refs[1][...] * parts[1]


def _epi_residual(parts, e_refs):
    return e_refs[0][...] + parts[0]


def _epi_swiglu(parts, e_refs):
    return _silu(parts[0]) * parts[1]


def attention_tables(n_pad):
    slopes = np.exp2(-8.0 * (np.arange(HEADS, dtype=np.float64) + 1.0) / HEADS) * LOG2_E
    i = np.arange(ATTN_ROWS)[:, None]
    j = np.arange(ATTN_ROWS + Q_BLOCK)[None, :]
    visible = (i - N_META) // CHUNK >= (j - N_META) // CHUNK
    rel = (i - np.abs(i - j)).astype(np.float64)
    diag = np.where(visible[None], slopes[:, None, None] * rel[None], NEG_BIG)
    col = slopes[:, None, None] * np.arange(n_pad, dtype=np.float64)[None, None, :]
    return jnp.asarray(diag, F32), jnp.asarray(col, F32)


def _attn_kernel(q_ref, k_ref, v_ref, lam_ref, dn_ref, diag_ref, col_ref, o_ref, kt_ref, *, lambda_init):
    n_pad = q_ref.shape[0]
    lp = lam_ref[...]
    s1 = jnp.sum(lp[0:1] * lp[1:2], axis=-1, keepdims=True)
    s2 = jnp.sum(lp[2:3] * lp[3:4], axis=-1, keepdims=True)
    lam = jnp.exp(s1) - jnp.exp(s2) + lambda_init
    for mp in range(2):
        kt_ref[mp] = k_ref[:, mp * QK_DIM:(mp + 1) * QK_DIM].T
    scale2 = QK_DIM ** -0.5 * LOG2_E
    gain = dn_ref[...] * (1.0 - lambda_init)
    for q0 in range(0, n_pad, ATTN_ROWS):
        qb = q0 // ATTN_ROWS
        n_rows = min(ATTN_ROWS, n_pad - q0)
        width = min(n_rows + Q_BLOCK, n_pad - q0)
        rows = slice(q0, q0 + n_rows)
        diag_bias = diag_ref[:n_rows, :width]
        past_bias = col_ref[:, :q0] - col_ref[:, q0:q0 + 1] if qb else None
        heads = []
        for mp in range(2):
            q = q_ref[rows, mp * QK_DIM:(mp + 1) * QK_DIM]
            s_d = jnp.dot(q, kt_ref[mp, :, q0:q0 + width], preferred_element_type=F32) * scale2 + diag_bias
            mx = jnp.max(s_d, axis=-1, keepdims=True)
            if qb:
                s_p = jnp.dot(q, kt_ref[mp, :, :q0], preferred_element_type=F32) * scale2 + past_bias
                mx = jnp.maximum(mx, jnp.max(s_p, axis=-1, keepdims=True))
            e_d = jnp.exp2(s_d - mx)
            den = jnp.sum(e_d, axis=-1, keepdims=True)
            pv = jnp.dot(e_d.astype(BF16), v_ref[q0:q0 + width, :], preferred_element_type=F32)
            if qb:
                e_p = jnp.exp2(s_p - mx)
                den = den + jnp.sum(e_p, axis=-1, keepdims=True)
                pv = pv + jnp.dot(e_p.astype(BF16), v_ref[:q0, :], preferred_element_type=F32)
            heads.append(pv / den)
        o = heads[0] - lam * heads[1]
        ms = jnp.mean(o * o, axis=-1, keepdims=True)
        o_ref[rows, :] = (o * lax.rsqrt(ms + SUBLN_EPS) * gain).astype(o_ref.dtype)


def diff_attention(qkv, lam_stack, diff_norm_stack, layer, diag, col, *, bsz, n_pad):
    m = qkv.shape[0]
    kernel = functools.partial(_attn_kernel, lambda_init=0.8 - 0.6 * math.exp(-0.3 * layer))
    return pl.pallas_call(
        kernel,
        grid=(bsz, HEADS),
        in_specs=[
            pl.BlockSpec((n_pad, V_DIM), lambda b, h: (b, h)),
            pl.BlockSpec((n_pad, V_DIM), lambda b, h: (b, HEADS + h)),
            pl.BlockSpec((n_pad, V_DIM), lambda b, h: (b, 2 * HEADS + h)),
            _layer_spec(4, QK_DIM, layer),
            _layer_spec(1, V_DIM, layer),
            pl.BlockSpec((None, ATTN_ROWS, ATTN_ROWS + Q_BLOCK), lambda b, h: (h, 0, 0)),
            pl.BlockSpec((None, 1, n_pad), lambda b, h: (h, 0, 0)),
        ],
        out_specs=pl.BlockSpec((n_pad, V_DIM), lambda b, h: (b, h)),
        out_shape=jax.ShapeDtypeStruct((m, ATTN_WIDTH), BF16),
        scratch_shapes=[pltpu.VMEM((2, QK_DIM, n_pad), BF16)],
        compiler_params=_compiler_params(("parallel", "parallel")),
    )(qkv, qkv, qkv, lam_stack, diff_norm_stack, diag, col)


def _split3_bf16(x):
    hi = x.astype(BF16)
    r1 = x - hi.astype(F32)
    mid = r1.astype(BF16)
    lo = (r1 - mid.astype(F32)).astype(BF16)
    return hi, mid, lo


def _ssd_kernel(zx_ref, dtr_ref, cw_ref, cb_ref, dtb_ref, alog_ref, dsk_ref, nw_ref, ex_ref, o_ref,
                state_ref, xbuf_ref, y_ref):
    c = pl.program_id(1)
    L = SSD_BLOCK
    halo = 8

    @pl.when(c == 0)
    def _():
        state_ref[...] = jnp.zeros_like(state_ref)
        xbuf_ref[0:halo, :] = jnp.zeros((halo, XBC_WIDTH), F32)

    @pl.when(c > 0)
    def _():
        xbuf_ref[0:halo, :] = xbuf_ref[L:L + halo, :]

    u = zx_ref[:, SSM_WIDTH:SSM_WIDTH + XBC_WIDTH]
    xbuf_ref[halo:halo + L, :] = u
    acc = cb_ref[...] + cw_ref[SSM_CONV - 1:SSM_CONV, :] * u
    for s in range(1, SSM_CONV):
        acc = acc + cw_ref[SSM_CONV - 1 - s:SSM_CONV - s, :] * xbuf_ref[halo - s:halo - s + L, :]
    xbc = _silu(acc)
    xs = xbc[:, :SSM_WIDTH]
    bm = xbc[:, SSM_WIDTH:SSM_WIDTH + SSM_GROUPS * SSM_STATE].astype(BF16)
    cm = xbc[:, SSM_WIDTH + SSM_GROUPS * SSM_STATE:].astype(BF16)

    pre = dtr_ref[...] + dtb_ref[...]
    dt = jnp.maximum(pre, 0.0) + jnp.log(1.0 + jnp.exp(-jnp.abs(pre)))
    adt = dt * (-jnp.exp(alog_ref[...]))
    rows = lax.broadcasted_iota(jnp.int32, (L, LANES), 0)
    cum = adt
    shift = 1
    while shift < L:
        cum = cum + jnp.where(rows >= shift, pltpu.roll(cum, shift, 0), 0.0)
        shift *= 2
    cum_last = cum[L - 1:L, :]
    ecum = jnp.exp(cum)
    edec = jnp.exp(cum_last - cum)
    elast = jnp.broadcast_to(jnp.exp(cum_last), (8, LANES))
    stacked = jnp.concatenate([dt, ecum, edec, elast], axis=0)
    ex = ex_ref[...]
    wide = sum(jnp.dot(p, ex, preferred_element_type=F32) for p in _split3_bf16(stacked))
    dt_w, ecum_w, edec_w, elast_w = wide[0:L], wide[L:2 * L], wide[2 * L:3 * L], wide[3 * L:3 * L + 1]

    xd = xs * dt_w
    xd_b = xd.astype(BF16)
    xdec_b = (xd * edec_w).astype(BF16)
    cum_t = cum.T
    tril = lax.broadcasted_iota(jnp.int32, (L, L), 0) >= lax.broadcasted_iota(jnp.int32, (L, L), 1)
    heads_per_group = SSM_HEADS // SSM_GROUPS
    for g in range(SSM_GROUPS):
        gs = slice(g * SSM_GROUP_WIDTH, (g + 1) * SSM_GROUP_WIDTH)
        bg = bm[:, g * SSM_STATE:(g + 1) * SSM_STATE]
        cg = cm[:, g * SSM_STATE:(g + 1) * SSM_STATE]
        cb = lax.dot_general(cg, bg, (((1,), (1,)), ((), ())), preferred_element_type=F32)
        st = state_ref[:, gs]
        y_off = jnp.dot(cg, st.astype(BF16), preferred_element_type=F32) * ecum_w[:, gs]
        state_ref[:, gs] = elast_w[:, gs] * st + lax.dot_general(
            bg, xdec_b[:, gs], (((0,), (0,)), ((), ())), preferred_element_type=F32)
        y_ref[:, gs] = y_off
        for r in range(heads_per_group):
            hd = g * heads_per_group + r
            hs = slice(hd * SSM_HEAD_DIM, (hd + 1) * SSM_HEAD_DIM)
            seg = cum[:, hd:hd + 1] - cum_t[hd:hd + 1, :]
            lmat = jnp.exp(jnp.where(tril, seg, -jnp.inf))
            y_ref[:, hs] += jnp.dot((cb * lmat).astype(BF16), xd_b[:, hs], preferred_element_type=F32)

    y = y_ref[...] + dsk_ref[...] * xs
    gated = y * _silu(zx_ref[:, 0:SSM_WIDTH])
    for g in range(SSM_GROUPS):
        gs = slice(g * SSM_GROUP_WIDTH, (g + 1) * SSM_GROUP_WIDTH)
        gg = gated[:, gs]
        ms = jnp.mean(gg * gg, axis=-1, keepdims=True)
        o_ref[:, gs] = (gg * lax.rsqrt(ms + NORM_EPS) * nw_ref[:, gs]).astype(o_ref.dtype)


def ssd_param_stacks(conv_b, dt_bias, a_log, d_skip, ssm_norm):
    n_layers = conv_b.shape[0]

    def pad_heads(v):
        return jnp.pad(v.astype(F32), ((0, 0), (0, LANES - SSM_HEADS))).reshape(n_layers, 1, LANES)

    return (conv_b.reshape(n_layers, 1, XBC_WIDTH), pad_heads(dt_bias), pad_heads(a_log),
            jnp.repeat(d_skip.astype(F32), SSM_HEAD_DIM, axis=1).reshape(n_layers, 1, SSM_WIDTH),
            ssm_norm.reshape(n_layers, 1, SSM_WIDTH))


def ssd_mixer(zx, dtr, conv_w, stacks, layer, expand, *, bsz, n_pad):
    m = zx.shape[0]
    nc = n_pad // SSD_BLOCK
    L = SSD_BLOCK
    return pl.pallas_call(
        _ssd_kernel,
        grid=(bsz, nc),
        in_specs=[
            pl.BlockSpec((L, ZX_COLS), lambda b, c: (b * nc + c, 0)),
            pl.BlockSpec((L, LANES), lambda b, c: (b * nc + c, 0)),
            _layer_spec(SSM_CONV, XBC_WIDTH, layer), _layer_spec(1, XBC_WIDTH, layer), _layer_spec(1, LANES, layer),
            _layer_spec(1, LANES, layer), _layer_spec(1, SSM_WIDTH, layer), _layer_spec(1, SSM_WIDTH, layer),
            pl.BlockSpec((LANES, SSM_WIDTH), lambda b, c: (0, 0)),
        ],
        out_specs=pl.BlockSpec((L, SSM_WIDTH), lambda b, c: (b * nc + c, 0)),
        out_shape=jax.ShapeDtypeStruct((m, SSM_WIDTH), BF16),
        scratch_shapes=[pltpu.VMEM((SSM_STATE, SSM_WIDTH), F32), pltpu.VMEM((L + 8, XBC_WIDTH), F32),
                        pltpu.VMEM((L, SSM_WIDTH), F32)],
        compiler_params=_compiler_params(("parallel", "arbitrary")),
    )(zx, dtr, conv_w, *stacks, expand)


def _dispatch_kernel(tok_ref, nused_ref, h_hbm, w_ref, o_ref, buf_ref, sem):
    i = pl.program_id(0)
    g = buf_ref.shape[1]
    nused = nused_ref[0]
    slot = i % 2

    def row_copy(src_row, dst_slot, r):
        return pltpu.make_async_copy(h_hbm.at[pl.ds(src_row, 1)], buf_ref.at[dst_slot, pl.ds(r, 1)],
                                     sem.at[dst_slot])

    def start_tile(tile, dst_slot):
        def body(r, carry):
            row_copy(tok_ref[tile * g + r], dst_slot, r).start()
            return carry
        lax.fori_loop(0, g, body, 0, unroll=8)

    def wait_tile(dst_slot):
        def body(r, carry):
            row_copy(0, dst_slot, r).wait()
            return carry
        lax.fori_loop(0, g, body, 0, unroll=8)

    @pl.when(jnp.logical_and(i == 0, nused > 0))
    def _():
        start_tile(0, 0)

    @pl.when(i + 1 < nused)
    def _():
        start_tile(i + 1, 1 - slot)

    @pl.when(i < nused)
    def _():
        wait_tile(slot)
        x = buf_ref[slot]
        ms = jnp.mean(x * x, axis=-1, keepdims=True)
        o_ref[...] = (x * lax.rsqrt(ms + NORM_EPS) * w_ref[...]).astype(o_ref.dtype)

    @pl.when(i >= nused)
    def _():
        o_ref[...] = jnp.zeros_like(o_ref)


def moe_dispatch(h, w_stack, layer, tok, nused, *, tg=MOE_ROW_TILE):
    d = h.shape[1]
    rows = tok.shape[0]
    return pl.pallas_call(
        _dispatch_kernel,
        grid_spec=pltpu.PrefetchScalarGridSpec(
            num_scalar_prefetch=2,
            grid=(rows // tg,),
            in_specs=[pl.BlockSpec(memory_space=pl.ANY), _layer_spec(1, d, layer)],
            out_specs=pl.BlockSpec((tg, d), lambda i, t, n: (i, 0)),
            scratch_shapes=[pltpu.VMEM((2, tg, d), F32), pltpu.SemaphoreType.DMA((2,))]),
        out_shape=jax.ShapeDtypeStruct((rows, d), BF16),
        compiler_params=_compiler_params(("arbitrary",)),
    )(tok, nused, h, w_stack)


def _combine_kernel(pos1_ref, pos2_ref, y_hbm, h_hbm, route_hbm, w_ref, o_ref,
                    y1_ref, y2_ref, h_ref, r_ref, sem_y, sem_h, *, n_pad, n_frames):
    i = pl.program_id(0)
    n_tiles = pl.num_programs(0)
    tr = h_ref.shape[1]
    tiles_per_batch = n_frames // tr
    slot = i % 2

    def block_copies(tile, dst_slot):
        b = tile // tiles_per_batch
        row0 = b * n_pad + N_META + (tile - b * tiles_per_batch) * tr
        return (pltpu.make_async_copy(h_hbm.at[pl.ds(row0, tr)], h_ref.at[dst_slot], sem_h.at[0, dst_slot]),
                pltpu.make_async_copy(route_hbm.at[pl.ds(row0, tr)], r_ref.at[dst_slot], sem_h.at[1, dst_slot]))

    def y_copies(row1, row2, dst_slot, r):
        return (pltpu.make_async_copy(y_hbm.at[pl.ds(row1, 1)], y1_ref.at[dst_slot, pl.ds(r, 1)], sem_y.at[dst_slot]),
                pltpu.make_async_copy(y_hbm.at[pl.ds(row2, 1)], y2_ref.at[dst_slot, pl.ds(r, 1)], sem_y.at[dst_slot]))

    def start_tile(tile, dst_slot):
        for cp in block_copies(tile, dst_slot):
            cp.start()

        def body(r, carry):
            t = tile * tr + r
            for cp in y_copies(pos1_ref[t], pos2_ref[t], dst_slot, r):
                cp.start()
            return carry
        lax.fori_loop(0, tr, body, 0, unroll=8)

    def wait_tile(tile, dst_slot):
        for cp in block_copies(tile, dst_slot):
            cp.wait()

        def body(r, carry):
            for cp in y_copies(0, 0, dst_slot, r):
                cp.wait()
            return carry
        lax.fori_loop(0, tr, body, 0, unroll=8)

    @pl.when(i == 0)
    def _():
        start_tile(0, 0)

    @pl.when(i + 1 < n_tiles)
    def _():
        start_tile(i + 1, 1 - slot)

    wait_tile(i, slot)
    route = r_ref[slot]
    lane = lax.broadcasted_iota(jnp.int32, route.shape, 1).astype(F32)
    in_experts = lane < N_EXPERTS
    w1 = jnp.sum(jnp.where(jnp.logical_and(in_experts, lane == route[:, N_EXPERTS:N_EXPERTS + 1]), route, 0.0),
                 axis=-1, keepdims=True)
    w2 = jnp.sum(jnp.where(jnp.logical_and(in_experts, lane == route[:, N_EXPERTS + 1:N_EXPERTS + 2]), route, 0.0),
                 axis=-1, keepdims=True)
    x = h_ref[slot] + w1 * y1_ref[slot] + w2 * y2_ref[slot]
    ms = jnp.mean(x * x, axis=-1, keepdims=True)
    o_ref[...] = x * lax.rsqrt(ms + NORM_EPS) * w_ref[...]


def moe_combine(ys, h, route, final_norm, pos1, pos2, *, bsz, n_pad, n_frames, tr=MOE_ROW_TILE):
    d = h.shape[1]
    kernel = functools.partial(_combine_kernel, n_pad=n_pad, n_frames=n_frames)
    any_spec = pl.BlockSpec(memory_space=pl.ANY)
    return pl.pallas_call(
        kernel,
        grid_spec=pltpu.PrefetchScalarGridSpec(
            num_scalar_prefetch=2,
            grid=(bsz * n_frames // tr,),
            in_specs=[any_spec, any_spec, any_spec, pl.BlockSpec((1, d), lambda i, p1, p2: (0, 0))],
            out_specs=pl.BlockSpec((tr, d), lambda i, p1, p2: (i, 0)),
            scratch_shapes=[pltpu.VMEM((2, tr, d), F32), pltpu.VMEM((2, tr, d), F32), pltpu.VMEM((2, tr, d), F32),
                            pltpu.VMEM((2, tr, LANES), F32), pltpu.SemaphoreType.DMA((2,)),
                            pltpu.SemaphoreType.DMA((2, 2))]),
        out_shape=jax.ShapeDtypeStruct((bsz * n_frames, d), F32),
        compiler_params=_compiler_params(("arbitrary",)),
    )(pos1, pos2, ys, h, route, final_norm.reshape(1, d))


def _moe_plan(route, *, bsz, n_pad, n_frames, tm):
    n_tok = bsz * n_frames
    cap = 2 * n_tok + N_EXPERTS * tm
    frames = route.reshape(bsz, n_pad, LANES)[:, N_META:N_META + n_frames].reshape(n_tok, LANES)
    idx = frames[:, N_EXPERTS:N_EXPERTS + 2].astype(jnp.int32)
    sel = (idx[:, :, None] == jnp.arange(N_EXPERTS)[None, None, :]).any(axis=1)
    counts = sel.sum(axis=0)
    padded = ((counts + tm - 1) // tm) * tm
    ends = jnp.cumsum(padded)
    starts = ends - padded
    rank = jnp.cumsum(sel, axis=0) - 1
    pos = starts[None, :] + rank
    pos12 = jnp.take_along_axis(pos, idx, axis=1).astype(jnp.int32)
    tok_rows = (jnp.arange(n_tok) // n_frames) * n_pad + N_META + jnp.arange(n_tok) % n_frames
    tok = jnp.zeros((cap,), jnp.int32).at[pos12.reshape(-1)].set(
        jnp.repeat(tok_rows.astype(jnp.int32), 2), mode="drop")
    tile_start = jnp.arange(cap // tm) * tm
    tile_grp = jnp.minimum((tile_start[:, None] >= ends[None, :]).sum(axis=1), N_EXPERTS - 1).astype(jnp.int32)
    rows_used = ends[-1].astype(jnp.int32).reshape(1)
    return tok, tile_grp, rows_used, pos12[:, 0], pos12[:, 1]


def _dense_plan(m, tm, group):
    return jnp.asarray(np.full((m // tm,), group, np.int32)), jnp.asarray(np.full((1,), m // tm, np.int32))


def mixer_tables(n_pad):
    diag, col = attention_tables(n_pad)
    expand = np.arange(LANES)[:, None] == (np.arange(SSM_WIDTH) // SSM_HEAD_DIM)[None, :]
    return {"attn_diag": diag, "attn_col": col, "expand": jnp.asarray(expand, BF16)}


def mixer_param_stacks(mix_norm, conv_b, dt_bias, a_log, d_skip, ssm_norm, lambda_q1, lambda_k1, lambda_q2, lambda_k2,
                       diff_norm):
    n_layers, d = mix_norm.shape
    return {
        "mix_norm": mix_norm.reshape(n_layers, 1, d),
        "ssd": ssd_param_stacks(conv_b, dt_bias, a_log, d_skip, ssm_norm),
        "lam": jnp.stack([lambda_q1, lambda_k1, lambda_q2, lambda_k2], axis=1).astype(F32),
        "diff_norm": diff_norm.reshape(n_layers, 1, V_DIM),
    }


def mixer_layer(h, layer, p, *, bsz, n_pad, tm, hn=None):
    m, d = h.shape
    grp, nused = _dense_plan(m, tm, layer)
    mm = functools.partial(grouped_matmul, grp=grp, nused=nused, tm=tm)
    if hn is None:
        hn = rmsnorm(h, p["mix_norm"], layer)
    mm_in = functools.partial(mm, [hn], [p["mix_w_in_t"]], [0], [], w_transposed=True)
    qkv = mm_in(_epi_plain, ncols=QKV_COLS, tn=512, out_dtype=BF16)
    zx = mm_in(_epi_plain, ncols=ZX_COLS, tn=512, out_dtype=F32, w_col0=(ZX_COL0,))
    dtr = mm_in(_epi_plain, ncols=LANES, tn=LANES, out_dtype=F32, w_col0=(DT_COL0,))
    gates = mm_in(_epi_sigmoid, ncols=2 * d, tn=512, out_dtype=BF16, w_col0=(DT_COL0,),
                  w_shift=(GATE_COL0 - DT_COL0,))
    o_attn = diff_attention(qkv, p["lam"], p["diff_norm"], layer, p["attn_diag"], p["attn_col"], bsz=bsz,
                            n_pad=n_pad)
    o_ssm = ssd_mixer(zx, dtr, p["conv_w"], p["ssd"], layer, p["expand"], bsz=bsz, n_pad=n_pad)
    merged = mm([o_attn, o_ssm], [p["w_proj_attn"], p["w_proj_ssm"]], [0, 1], [gates, gates], _epi_gated_sum,
                ncols=d, tn=512, out_dtype=BF16, extra_col0=(0, d))
    return mm([merged], [p["w_out"]], [0], [h], _epi_residual, ncols=d, tn=512, out_dtype=F32, extra_col0=(0,))


def dense_ffn(h, norm_stack, layer, w_gate, w_up, w_down, group, *, tm):
    m, d = h.shape
    d_ff = w_gate.shape[2]
    grp, nused = _dense_plan(m, tm, group)
    mm = functools.partial(grouped_matmul, grp=grp, nused=nused, tm=tm)
    hn = rmsnorm(h, norm_stack, layer)
    u = mm([hn], [w_gate, w_up], [0, 0], [], _epi_swiglu, ncols=d_ff, tn=256, out_dtype=BF16)
    tm_down = tm // 2
    grp, nused = _dense_plan(m, tm_down, group)
    return grouped_matmul([u], [w_down.astype(BF16)], [0], [h], _epi_residual, grp=grp, nused=nused, ncols=d,
                          tm=tm_down, tn=512, out_dtype=F32, extra_col0=(0,))


def moe_ffn_final(h, norm_stack, layer, router_w, w_gate, w_up, w_down, final_norm, group, *, bsz, n_pad, n_frames):
    d = h.shape[1]
    d_fe = w_gate.shape[3]
    r_pad = jnp.pad(router_w.astype(F32), ((0, 0), (0, LANES - N_EXPERTS)))
    route = route_tokens(h, norm_stack, layer, r_pad)
    tok, tile_grp, rows_used, pos1, pos2 = _moe_plan(route, bsz=bsz, n_pad=n_pad, n_frames=n_frames,
                                                      tm=MOE_MM_ROW_TILE)
    xs = moe_dispatch(h, norm_stack, layer, tok, rows_used // MOE_ROW_TILE)
    mm = functools.partial(grouped_matmul, grp=tile_grp + group * N_EXPERTS, nused=rows_used // MOE_MM_ROW_TILE,
                           tm=MOE_MM_ROW_TILE)
    u = mm([xs], [w_gate.reshape(-1, d, d_fe), w_up.reshape(-1, d, d_fe)], [0, 0], [], _epi_swiglu,
           ncols=d_fe, tn=512, out_dtype=BF16)
    ys = mm([u], [w_down.reshape(-1, d_fe, d)], [0], [], _epi_plain, ncols=d, tn=1024, out_dtype=F32)
    return moe_combine(ys, h, route, final_norm, pos1, pos2, bsz=bsz, n_pad=n_pad, n_frames=n_frames)


def kernel(x, meta_tokens, mix_norm, mix_w_in, conv_w, conv_b, dt_bias, a_log, d_skip, ssm_norm, lambda_q1, lambda_k1, lambda_q2, lambda_k2, diff_norm, w_proj_attn, w_proj_ssm, w_out, ffn_norm, ffn_w_gate, ffn_w_up, ffn_w_down, router, moe_w_gate, moe_w_up, moe_w_down, final_norm):
    bsz, n_frames, d = x.shape
    assert mix_norm.shape[0] == 2 and d == D_MODEL, "layer 0: dense FFN; layer 1: MoE FFN, then the final norm"
    n = N_META + n_frames
    n_pad = -(-n // Q_BLOCK) * Q_BLOCK
    m = bsz * n_pad
    tm = n_pad // 2
    assert tm % 16 == 0 and n_pad % SSD_BLOCK == 0 and n_frames % MOE_ROW_TILE == 0

    p = dict(mix_w_in_t=jnp.swapaxes(mix_w_in, 1, 2), conv_w=conv_w, w_proj_attn=w_proj_attn, w_proj_ssm=w_proj_ssm,
             w_out=w_out)
    p.update(mixer_param_stacks(mix_norm, conv_b, dt_bias, a_log, d_skip, ssm_norm, lambda_q1, lambda_k1, lambda_q2,
                                lambda_k2, diff_norm))
    p.update(mixer_tables(n_pad))
    ffn_norm_stack = ffn_norm.reshape(ffn_norm.shape[0], 1, d)
    h, hn = embed_norm(x, meta_tokens, p["mix_norm"], 0)
    assert h.shape == (m, d)
    h = mixer_layer(h, 0, p, bsz=bsz, n_pad=n_pad, tm=tm, hn=hn)
    h = dense_ffn(h, ffn_norm_stack, 0, ffn_w_gate, ffn_w_up, ffn_w_down, 0, tm=tm)
    h = mixer_layer(h, 1, p, bsz=bsz, n_pad=n_pad, tm=tm)
    out = moe_ffn_final(h, ffn_norm_stack, 1, router[0], moe_w_gate, moe_w_up, moe_w_down, final_norm, 0,
                        bsz=bsz, n_pad=n_pad, n_frames=n_frames)
    return out.reshape(bsz, n_frames, d)
```

```python
import functools
import math

import jax
import jax.numpy as jnp
import numpy as np
from jax import lax
from jax.experimental import pallas as pl
from jax.experimental.pallas import tpu as pltpu

F32 = jnp.float32
BF16 = jnp.bfloat16

LANES = 128
VMEM_LIMIT_BYTES_V7X = 56 * 2**20

D_MODEL = 4096
N_META = 16
CHUNK = 64
Q_BLOCK = 128
ATTN_ROWS = 2 * Q_BLOCK
HEADS = 8
QK_DIM = 128
V_DIM = 256
ATTN_WIDTH = HEADS * V_DIM
SUBLN_EPS = 1e-5
SSM_WIDTH = 2048
SSM_HEAD_DIM = 64
SSM_HEADS = 32
SSM_GROUPS = 4
SSM_STATE = 128
SSM_CONV = 4
SSM_GROUP_WIDTH = SSM_WIDTH // SSM_GROUPS
XBC_WIDTH = SSM_WIDTH + 2 * SSM_GROUPS * SSM_STATE
QKV_COLS = 3 * ATTN_WIDTH
ZX_COL0 = QKV_COLS
ZX_COLS = SSM_WIDTH + XBC_WIDTH
DT_COL0 = ZX_COL0 + ZX_COLS
GATE_COL0 = DT_COL0 + SSM_HEADS
N_EXPERTS = 8
NORM_EPS = 1e-6
NEG_BIG = -1e30
LOG2_E = 1.4426950408889634
SSD_BLOCK = 128

MOE_ROW_TILE = 256
MOE_MM_ROW_TILE = 512
NORM_ROW_TILE = 272


def _sigmoid(x):
    return 0.5 * jnp.tanh(0.5 * x) + 0.5


def _silu(x):
    return x * _sigmoid(x)


def _compiler_params(semantics):
    return pltpu.CompilerParams(dimension_semantics=semantics, vmem_limit_bytes=VMEM_LIMIT_BYTES_V7X)


def _rmsnorm_kernel(x_ref, w_ref, o_ref, *, eps):
    x = x_ref[...]
    ms = jnp.mean(x * x, axis=-1, keepdims=True)
    o_ref[...] = (x * lax.rsqrt(ms + eps) * w_ref[...]).astype(o_ref.dtype)


def _norm_row_tile(m):
    return NORM_ROW_TILE if m % NORM_ROW_TILE == 0 else Q_BLOCK


def _layer_spec(rows, width, layer):
    return pl.BlockSpec((None, rows, width), lambda *_: (layer, 0, 0))


def rmsnorm(x, w_stack, layer, *, eps=NORM_EPS, out_dtype=BF16):
    m, d = x.shape
    tr = _norm_row_tile(m)
    return pl.pallas_call(
        functools.partial(_rmsnorm_kernel, eps=eps),
        grid=(m // tr,),
        in_specs=[pl.BlockSpec((tr, d), lambda i: (i, 0)), _layer_spec(1, d, layer)],
        out_specs=pl.BlockSpec((tr, d), lambda i: (i, 0)),
        out_shape=jax.ShapeDtypeStruct((m, d), out_dtype),
        compiler_params=_compiler_params(("parallel",)),
    )(x, w_stack)


def _embed_norm_kernel(x_ref, meta_ref, w_ref, h_ref, hn_ref, *, eps):
    c = pl.program_id(1)
    last = pl.num_programs(1) - 1
    tail = Q_BLOCK - N_META

    def emit(rows):
        h_ref[...] = rows
        ms = jnp.mean(rows * rows, axis=-1, keepdims=True)
        hn_ref[...] = (rows * lax.rsqrt(ms + eps) * w_ref[...]).astype(hn_ref.dtype)

    @pl.when(c == 0)
    def _():
        emit(jnp.concatenate([meta_ref[...], x_ref[0, 0:tail, :]], axis=0))

    @pl.when(jnp.logical_and(c > 0, c < last))
    def _():
        emit(x_ref[0])

    @pl.when(c == last)
    def _():
        emit(jnp.concatenate([x_ref[0, tail:Q_BLOCK, :], jnp.zeros((tail, x_ref.shape[2]), F32)], axis=0))


def embed_norm(x, meta_tokens, w_stack, layer):
    bsz, n_frames, d = x.shape
    assert n_frames % Q_BLOCK == 0 and meta_tokens.shape == (N_META, d)
    nc = n_frames // Q_BLOCK + 1
    m = bsz * nc * Q_BLOCK

    def x_index(b, c):
        start = jnp.clip(c * Q_BLOCK - N_META, 0, n_frames - Q_BLOCK)
        return b, pl.multiple_of(start, N_META), 0

    row_spec = pl.BlockSpec((Q_BLOCK, d), lambda b, c: (b * nc + c, 0))
    return pl.pallas_call(
        functools.partial(_embed_norm_kernel, eps=NORM_EPS),
        grid=(bsz, nc),
        in_specs=[pl.BlockSpec((pl.Element(1), pl.Element(Q_BLOCK), pl.Element(d)), x_index),
                  pl.BlockSpec((N_META, d), lambda b, c: (0, 0)), _layer_spec(1, d, layer)],
        out_specs=[row_spec, row_spec],
        out_shape=[jax.ShapeDtypeStruct((m, d), F32), jax.ShapeDtypeStruct((m, d), BF16)],
        compiler_params=_compiler_params(("parallel", "arbitrary")),
    )(x, meta_tokens.astype(F32), w_stack)


def _split_bf16(x):
    hi = x.astype(BF16)
    lo = (x - hi.astype(F32)).astype(BF16)
    return hi, lo


def _router_kernel(x_ref, w_ref, r_ref, o_ref, *, eps):
    x = x_ref[...]
    ms = jnp.mean(x * x, axis=-1, keepdims=True)
    hn = x * lax.rsqrt(ms + eps) * w_ref[...]
    h_hi, h_lo = _split_bf16(hn)
    r_hi, r_lo = _split_bf16(r_ref[...])
    logits = (jnp.dot(h_hi, r_hi, preferred_element_type=F32)
              + jnp.dot(h_hi, r_lo, preferred_element_type=F32)
              + jnp.dot(h_lo, r_hi, preferred_element_type=F32))
    lane = lax.broadcasted_iota(jnp.int32, logits.shape, 1).astype(F32)
    lg = jnp.where(lane < N_EXPERTS, logits, -jnp.inf)
    m1 = jnp.max(lg, axis=-1, keepdims=True)
    i1 = jnp.min(jnp.where(lg == m1, lane, float(LANES)), axis=-1, keepdims=True)
    lg2 = jnp.where(lane == i1, -jnp.inf, lg)
    m2 = jnp.max(lg2, axis=-1, keepdims=True)
    i2 = jnp.min(jnp.where(lg2 == m2, lane, float(LANES)), axis=-1, keepdims=True)
    e2 = jnp.exp(m2 - m1)
    den = 1.0 + e2
    out = jnp.where(lane == i1, 1.0 / den, 0.0) + jnp.where(lane == i2, e2 / den, 0.0)
    out = jnp.where(lane == N_EXPERTS, i1, out)
    out = jnp.where(lane == N_EXPERTS + 1, i2, out)
    o_ref[...] = out


def route_tokens(x, w_stack, layer, r_pad):
    m, d = x.shape
    tr = _norm_row_tile(m)
    return pl.pallas_call(
        functools.partial(_router_kernel, eps=NORM_EPS),
        grid=(m // tr,),
        in_specs=[pl.BlockSpec((tr, d), lambda i: (i, 0)), _layer_spec(1, d, layer),
                  pl.BlockSpec((d, LANES), lambda i: (0, 0))],
        out_specs=pl.BlockSpec((tr, LANES), lambda i: (i, 0)),
        out_shape=jax.ShapeDtypeStruct((m, LANES), F32),
        compiler_params=_compiler_params(("parallel",)),
    )(x, w_stack, r_pad)


def _mm_kernel(grp_ref, nused_ref, *refs, n_a, w_to_a, cast, w_shift, w_transposed, n_extra, nk, epilogue):
    col_axis = 0 if w_transposed else 1
    n_w = len(w_to_a)
    a_refs = refs[:n_a]
    w_refs = refs[n_a:n_a + n_w]
    refs = list(refs[n_a + n_w:])
    wnext_refs = [refs.pop(0) if s else None for s in w_shift]
    e_refs = [refs.pop(0) for _ in range(n_extra)]
    o_ref = refs.pop(0)
    scratch = refs
    wb_refs = [scratch.pop(0) if c else None for c in cast]
    acc_refs = [scratch.pop(0) for _ in range(n_w)] if nk > 1 else []
    i = pl.program_id(1)
    k = pl.program_id(2)

    @pl.when(i >= nused_ref[0])
    def _():
        o_ref[...] = jnp.zeros_like(o_ref)

    @pl.when(i < nused_ref[0])
    def _():
        if any(cast):
            def do_cast():
                for w_ref, wn_ref, wb_ref, s in zip(w_refs, wnext_refs, wb_refs, w_shift):
                    if wb_ref is None:
                        continue
                    if s:
                        tn = w_ref.shape[col_axis]
                        wide = jnp.concatenate([w_ref[...], wn_ref[...]], axis=col_axis)
                        wide = wide[s:s + tn, :] if w_transposed else wide[:, s:s + tn]
                        wb_ref[...] = wide.astype(BF16)
                    else:
                        wb_ref[...] = w_ref[...].astype(BF16)
            if nk == 1:
                changed = jnp.logical_or(i == 0, grp_ref[i] != grp_ref[jnp.maximum(i - 1, 0)])
                pl.when(changed)(do_cast)
            else:
                do_cast()
        parts = []
        for wi in range(n_w):
            w = wb_refs[wi][...] if cast[wi] else w_refs[wi][...]
            contract = (((1,), (1,)), ((), ())) if w_transposed else (((1,), (0,)), ((), ()))
            parts.append(lax.dot_general(a_refs[w_to_a[wi]][...], w, contract, preferred_element_type=F32))
        if nk == 1:
            o_ref[...] = epilogue(parts, e_refs).astype(o_ref.dtype)
        else:
            @pl.when(k == 0)
            def _():
                for acc, p in zip(acc_refs, parts):
                    acc[...] = p

            @pl.when(k > 0)
            def _():
                for acc, p in zip(acc_refs, parts):
                    acc[...] += p

            @pl.when(k == nk - 1)
            def _():
                o_ref[...] = epilogue([acc[...] for acc in acc_refs], e_refs).astype(o_ref.dtype)


def grouped_matmul(a_list, w_list, w_to_a, extras, epilogue, *, grp, nused, ncols, tm, tn, out_dtype,
                   w_col0=None, w_shift=None, w_transposed=False, extra_col0=(), tk=None):
    m, kdim = a_list[0].shape
    tk = kdim if tk is None else tk
    nk = kdim // tk
    w_col0 = (0,) * len(w_list) if w_col0 is None else tuple(w_col0)
    w_shift = (0,) * len(w_list) if w_shift is None else tuple(w_shift)
    assert m % tm == 0 and ncols % tn == 0 and kdim % tk == 0 and all(c % tn == 0 for c in w_col0)
    assert all(a.shape == (m, kdim) for a in a_list)
    assert all(w.shape[2 if w_transposed else 1] == kdim for w in w_list)
    cast = tuple(w.dtype != BF16 for w in w_list)
    assert all(c or not s for c, s in zip(cast, w_shift)) and all(0 <= s < LANES for s in w_shift)
    assert not w_transposed or all(s % 8 == 0 for s in w_shift)
    ecb = tuple(c // tn for c in extra_col0)
    lane_tiles = tn // LANES

    def row(i, n):
        return jnp.minimum(i, n[0] - 1)

    def w_block(width, col_index):
        if w_transposed:
            return pl.BlockSpec((None, width, tk), lambda j, i, k, g, n: (g[row(i, n)], col_index(j), k))
        return pl.BlockSpec((None, tk, width), lambda j, i, k, g, n: (g[row(i, n)], k, col_index(j)))

    def w_spec(cb):
        return w_block(tn, lambda j: cb + j)

    def w_next_spec(cb):
        return w_block(LANES, lambda j: (cb + j + 1) * lane_tiles)

    in_specs = [pl.BlockSpec((tm, tk), lambda j, i, k, g, n: (row(i, n), k)) for _ in a_list]
    in_specs += [w_spec(c // tn) for c in w_col0]
    in_specs += [w_next_spec(c // tn) for c, s in zip(w_col0, w_shift) if s]
    in_specs += [pl.BlockSpec((tm, tn), functools.partial(lambda j, i, k, g, n, eb: (row(i, n), eb + j), eb=eb))
                 for eb in ecb]
    scratch = [pltpu.VMEM((tn, tk) if w_transposed else (tk, tn), BF16) for c in cast if c]
    if nk > 1:
        scratch += [pltpu.VMEM((tm, tn), F32) for _ in w_list]
    kernel = functools.partial(_mm_kernel, n_a=len(a_list), w_to_a=tuple(w_to_a), cast=cast, w_shift=w_shift,
                               w_transposed=w_transposed, n_extra=len(extras), nk=nk, epilogue=epilogue)
    w_next = [w for w, s in zip(w_list, w_shift) if s]
    return pl.pallas_call(
        kernel,
        grid_spec=pltpu.PrefetchScalarGridSpec(
            num_scalar_prefetch=2,
            grid=(ncols // tn, m // tm, nk),
            in_specs=in_specs,
            out_specs=pl.BlockSpec((tm, tn), lambda j, i, k, g, n: (i, j)),
            scratch_shapes=scratch),
        out_shape=jax.ShapeDtypeStruct((m, ncols), out_dtype),
        compiler_params=_compiler_params(("arbitrary", "arbitrary", "arbitrary")),
    )(grp, nused, *a_list, *w_list, *w_next, *extras)


def _epi_plain(parts, e_refs):
    return parts[0]


def _epi_sigmoid(parts, e_refs):
    return _sigmoid(parts[0])


def _epi_gated_sum(parts, e_refs):
    return e_refs[0][...] * parts[0] + e_refs[1][...] * parts[1]


def _epi_residual(parts, e_refs):
    return e_refs[0][...] + parts[0]


def _epi_swiglu(parts, e_refs):
    return _silu(parts[0]) * parts[1]


def attention_tables(n_pad):
    slopes = np.exp2(-8.0 * (np.arange(HEADS, dtype=np.float64) + 1.0) / HEADS) * LOG2_E
    i = np.arange(ATTN_ROWS)[:, None]
    j = np.arange(ATTN_ROWS + Q_BLOCK)[None, :]
    visible = (i - N_META) // CHUNK >= (j - N_META) // CHUNK
    rel = (i - np.abs(i - j)).astype(np.float64)
    diag = np.where(visible[None], slopes[:, None, None] * rel[None], NEG_BIG)
    col = slopes[:, None, None] * np.arange(n_pad, dtype=np.float64)[None, None, :]
    return jnp.asarray(diag, F32), jnp.asarray(col, F32)


def _attn_kernel(q_ref, k_ref, v_ref, lam_ref, dn_ref, diag_ref, col_ref, o_ref, kt_ref, *, lambda_init):
    n_pad = q_ref.shape[0]
    lp = lam_ref[...]
    s1 = jnp.sum(lp[0:1] * lp[1:2], axis=-1, keepdims=True)
    s2 = jnp.sum(lp[2:3] * lp[3:4], axis=-1, keepdims=True)
    lam = jnp.exp(s1) - jnp.exp(s2) + lambda_init
    for mp in range(2):
        kt_ref[mp] = k_ref[:, mp * QK_DIM:(mp + 1) * QK_DIM].T
    scale2 = QK_DIM ** -0.5 * LOG2_E
    gain = dn_ref[...] * (1.0 - lambda_init)
    for q0 in range(0, n_pad, ATTN_ROWS):
        qb = q0 // ATTN_ROWS
        n_rows = min(ATTN_ROWS, n_pad - q0)
        width = min(n_rows + Q_BLOCK, n_pad - q0)
        rows = slice(q0, q0 + n_rows)
        diag_bias = diag_ref[:n_rows, :width]
        past_bias = col_ref[:, :q0] - col_ref[:, q0:q0 + 1] if qb else None
        heads = []
        for mp in range(2):
            q = q_ref[rows, mp * QK_DIM:(mp + 1) * QK_DIM]
            s_d = jnp.dot(q, kt_ref[mp, :, q0:q0 + width], preferred_element_type=F32) * scale2 + diag_bias
            mx = jnp.max(s_d, axis=-1, keepdims=True)
            if qb:
                s_p = jnp.dot(q, kt_ref[mp, :, :q0], preferred_element_type=F32) * scale2 + past_bias
                mx = jnp.maximum(mx, jnp.max(s_p, axis=-1, keepdims=True))
            e_d = jnp.exp2(s_d - mx)
            den = jnp.sum(e_d, axis=-1, keepdims=True)
            pv = jnp.dot(e_d.astype(BF16), v_ref[q0:q0 + width, :], preferred_element_type=F32)
            if qb:
                e_p = jnp.exp2(s_p - mx)
                den = den + jnp.sum(e_p, axis=-1, keepdims=True)
                pv = pv + jnp.dot(e_p.astype(BF16), v_ref[:q0, :], preferred_element_type=F32)
            heads.append(pv / den)
        o = heads[0] - lam * heads[1]
        ms = jnp.mean(o * o, axis=-1, keepdims=True)
        o_ref[rows, :] = (o * lax.rsqrt(ms + SUBLN_EPS) * gain).astype(o_ref.dtype)


def diff_attention(qkv, lam_stack, diff_norm_stack, layer, diag, col, *, bsz, n_pad):
    m = qkv.shape[0]
    kernel = functools.partial(_attn_kernel, lambda_init=0.8 - 0.6 * math.exp(-0.3 * layer))
    return pl.pallas_call(
        kernel,
        grid=(bsz, HEADS),
        in_specs=[
            pl.BlockSpec((n_pad, V_DIM), lambda b, h: (b, h)),
            pl.BlockSpec((n_pad, V_DIM), lambda b, h: (b, HEADS + h)),
            pl.BlockSpec((n_pad, V_DIM), lambda b, h: (b, 2 * HEADS + h)),
            _layer_spec(4, QK_DIM, layer),
            _layer_spec(1, V_DIM, layer),
            pl.BlockSpec((None, ATTN_ROWS, ATTN_ROWS + Q_BLOCK), lambda b, h: (h, 0, 0)),
            pl.BlockSpec((None, 1, n_pad), lambda b, h: (h, 0, 0)),
        ],
        out_specs=pl.BlockSpec((n_pad, V_DIM), lambda b, h: (b, h)),
        out_shape=jax.ShapeDtypeStruct((m, ATTN_WIDTH), BF16),
        scratch_shapes=[pltpu.VMEM((2, QK_DIM, n_pad), BF16)],
        compiler_params=_compiler_params(("parallel", "parallel")),
    )(qkv, qkv, qkv, lam_stack, diff_norm_stack, diag, col)


def _ssd_kernel(zx_ref, dtr_ref, cw_ref, cb_ref, dtb_ref, alog_ref, dsk_ref, nw_ref, ex_ref, o_ref,
                state_ref, xbuf_ref, y_ref):
    c = pl.program_id(1)
    L = SSD_BLOCK
    halo = 8

    @pl.when(c == 0)
    def _():
        state_ref[...] = jnp.zeros_like(state_ref)
        xbuf_ref[0:halo, :] = jnp.zeros((halo, XBC_WIDTH), F32)

    @pl.when(c > 0)
    def _():
        xbuf_ref[0:halo, :] = xbuf_ref[L:L + halo, :]

    u = zx_ref[:, SSM_WIDTH:SSM_WIDTH + XBC_WIDTH]
    xbuf_ref[halo:halo + L, :] = u
    acc = cb_ref[...] + cw_ref[SSM_CONV - 1:SSM_CONV, :] * u
    for s in range(1, SSM_CONV):
        acc = acc + cw_ref[SSM_CONV - 1 - s:SSM_CONV - s, :] * xbuf_ref[halo - s:halo - s + L, :]
    xbc = _silu(acc)
    xs = xbc[:, :SSM_WIDTH]
    bm = xbc[:, SSM_WIDTH:SSM_WIDTH + SSM_GROUPS * SSM_STATE].astype(BF16)
    cm = xbc[:, SSM_WIDTH + SSM_GROUPS * SSM_STATE:].astype(BF16)

    pre = dtr_ref[...] + dtb_ref[...]
    dt = jnp.maximum(pre, 0.0) + jnp.log(1.0 + jnp.exp(-jnp.abs(pre)))
    adt = dt * (-jnp.exp(alog_ref[...]))
    rows = lax.broadcasted_iota(jnp.int32, (L, LANES), 0)
    cum = adt
    shift = 1
    while shift < L:
        cum = cum + jnp.where(rows >= shift, pltpu.roll(cum, shift, 0), 0.0)
        shift *= 2
    cum_last = cum[L - 1:L, :]
    ecum = jnp.exp(cum)
    edec = jnp.exp(cum_last - cum)
    elast = jnp.broadcast_to(jnp.exp(cum_last), (8, LANES))
    stacked = jnp.concatenate([dt, ecum, edec, elast], axis=0)
    ex = ex_ref[...]
    wide = sum(jnp.dot(p, ex, preferred_element_type=F32) for p in _split_bf16(stacked))
    dt_w, ecum_w, edec_w, elast_w = wide[0:L], wide[L:2 * L], wide[2 * L:3 * L], wide[3 * L:3 * L + 1]

    xd = xs * dt_w
    xd_b = xd.astype(BF16)
    xdec_b = (xd * edec_w).astype(BF16)
    cum_t = cum.T
    tril = lax.broadcasted_iota(jnp.int32, (L, L), 0) >= lax.broadcasted_iota(jnp.int32, (L, L), 1)
    heads_per_group = SSM_HEADS // SSM_GROUPS
    for g in range(SSM_GROUPS):
        gs = slice(g * SSM_GROUP_WIDTH, (g + 1) * SSM_GROUP_WIDTH)
        bg = bm[:, g * SSM_STATE:(g + 1) * SSM_STATE]
        cg = cm[:, g * SSM_STATE:(g + 1) * SSM_STATE]
        cb = lax.dot_general(cg, bg, (((1,), (1,)), ((), ())), preferred_element_type=F32)
        st = state_ref[:, gs]
        y_off = jnp.dot(cg, st.astype(BF16), preferred_element_type=F32) * ecum_w[:, gs]
        state_ref[:, gs] = elast_w[:, gs] * st + lax.dot_general(
            bg, xdec_b[:, gs], (((0,), (0,)), ((), ())), preferred_element_type=F32)
        y_ref[:, gs] = y_off
        for r in range(heads_per_group):
            hd = g * heads_per_group + r
            hs = slice(hd * SSM_HEAD_DIM, (hd + 1) * SSM_HEAD_DIM)
            seg = cum[:, hd:hd + 1] - cum_t[hd:hd + 1, :]
            lmat = jnp.exp(jnp.where(tril, seg, -jnp.inf))
            y_ref[:, hs] += jnp.dot((cb * lmat).astype(BF16), xd_b[:, hs], preferred_element_type=F32)

    y = y_ref[...] + dsk_ref[...] * xs
    gated = y * _silu(zx_ref[:, 0:SSM_WIDTH])
    for g in range(SSM_GROUPS):
        gs = slice(g * SSM_GROUP_WIDTH, (g + 1) * SSM_GROUP_WIDTH)
        gg = gated[:, gs]
        ms = jnp.mean(gg * gg, axis=-1, keepdims=True)
        o_ref[:, gs] = (gg * lax.rsqrt(ms + NORM_EPS) * nw_ref[:, gs]).astype(o_ref.dtype)


def ssd_param_stacks(conv_b, dt_bias, a_log, d_skip, ssm_norm):
    n_layers = conv_b.shape[0]

    def pad_heads(v):
        return jnp.pad(v.astype(F32), ((0, 0), (0, LANES - SSM_HEADS))).reshape(n_layers, 1, LANES)

    return (conv_b.reshape(n_layers, 1, XBC_WIDTH), pad_heads(dt_bias), pad_heads(a_log),
            jnp.repeat(d_skip.astype(F32), SSM_HEAD_DIM, axis=1).reshape(n_layers, 1, SSM_WIDTH),
            ssm_norm.reshape(n_layers, 1, SSM_WIDTH))


def ssd_mixer(zx, dtr, conv_w, stacks, layer, expand, *, bsz, n_pad):
    m = zx.shape[0]
    nc = n_pad // SSD_BLOCK
    L = SSD_BLOCK
    return pl.pallas_call(
        _ssd_kernel,
        grid=(bsz, nc),
        in_specs=[
            pl.BlockSpec((L, ZX_COLS), lambda b, c: (b * nc + c, 0)),
            pl.BlockSpec((L, LANES), lambda b, c: (b * nc + c, 0)),
            _layer_spec(SSM_CONV, XBC_WIDTH, layer), _layer_spec(1, XBC_WIDTH, layer), _layer_spec(1, LANES, layer),
            _layer_spec(1, LANES, layer), _layer_spec(1, SSM_WIDTH, layer), _layer_spec(1, SSM_WIDTH, layer),
            pl.BlockSpec((LANES, SSM_WIDTH), lambda b, c: (0, 0)),
        ],
        out_specs=pl.BlockSpec((L, SSM_WIDTH), lambda b, c: (b * nc + c, 0)),
        out_shape=jax.ShapeDtypeStruct((m, SSM_WIDTH), BF16),
        scratch_shapes=[pltpu.VMEM((SSM_STATE, SSM_WIDTH), F32), pltpu.VMEM((L + 8, XBC_WIDTH), F32),
                        pltpu.VMEM((L, SSM_WIDTH), F32)],
        compiler_params=_compiler_params(("parallel", "arbitrary")),
    )(zx, dtr, conv_w, *stacks, expand)


def _dispatch_kernel(tok_ref, nused_ref, h_hbm, w_ref, o_ref, buf_ref, sem):
    i = pl.program_id(0)
    g = buf_ref.shape[1]
    nused = nused_ref[0]
    slot = i % 2

    def row_copy(src_row, dst_slot, r):
        return pltpu.make_async_copy(h_hbm.at[pl.ds(src_row, 1)], buf_ref.at[dst_slot, pl.ds(r, 1)],
                                     sem.at[dst_slot])

    def start_tile(tile, dst_slot):
        def body(r, carry):
            row_copy(tok_ref[tile * g + r], dst_slot, r).start()
            return carry
        lax.fori_loop(0, g, body, 0, unroll=8)

    def wait_tile(dst_slot):
        def body(r, carry):
            row_copy(0, dst_slot, r).wait()
            return carry
        lax.fori_loop(0, g, body, 0, unroll=8)

    @pl.when(jnp.logical_and(i == 0, nused > 0))
    def _():
        start_tile(0, 0)

    @pl.when(i + 1 < nused)
    def _():
        start_tile(i + 1, 1 - slot)

    @pl.when(i < nused)
    def _():
        wait_tile(slot)
        x = buf_ref[slot]
        ms = jnp.mean(x * x, axis=-1, keepdims=True)
        o_ref[...] = (buf_ref[slot] * lax.rsqrt(ms + NORM_EPS) * w_ref[...]).astype(o_ref.dtype)

    @pl.when(i >= nused)
    def _():
        o_ref[...] = jnp.zeros_like(o_ref)


def moe_dispatch(h, w_stack, layer, tok, nused, *, tg=MOE_ROW_TILE):
    d = h.shape[1]
    rows = tok.shape[0]
    return pl.pallas_call(
        _dispatch_kernel,
        grid_spec=pltpu.PrefetchScalarGridSpec(
            num_scalar_prefetch=2,
            grid=(rows // tg,),
            in_specs=[pl.BlockSpec(memory_space=pl.ANY), _layer_spec(1, d, layer)],
            out_specs=pl.BlockSpec((tg, d), lambda i, t, n: (i, 0)),
            scratch_shapes=[pltpu.VMEM((2, tg, d), F32), pltpu.SemaphoreType.DMA((2,))]),
        out_shape=jax.ShapeDtypeStruct((rows, d), BF16),
        compiler_params=_compiler_params(("arbitrary",)),
    )(tok, nused, h, w_stack)


def _combine_kernel(pos1_ref, pos2_ref, y_hbm, h_hbm, route_hbm, w_ref, o_ref,
                    y1_ref, y2_ref, h_ref, r_ref, sem_y, sem_h, *, n_pad, n_frames):
    i = pl.program_id(0)
    n_tiles = pl.num_programs(0)
    tr = h_ref.shape[1]
    tiles_per_batch = n_frames // tr
    slot = i % 2

    def block_copies(tile, dst_slot):
        b = tile // tiles_per_batch
        row0 = b * n_pad + N_META + (tile - b * tiles_per_batch) * tr
        return (pltpu.make_async_copy(h_hbm.at[pl.ds(row0, tr)], h_ref.at[dst_slot], sem_h.at[0, dst_slot]),
                pltpu.make_async_copy(route_hbm.at[pl.ds(row0, tr)], r_ref.at[dst_slot], sem_h.at[1, dst_slot]))

    def y_copies(row1, row2, dst_slot, r):
        return (pltpu.make_async_copy(y_hbm.at[pl.ds(row1, 1)], y1_ref.at[dst_slot, pl.ds(r, 1)], sem_y.at[dst_slot]),
                pltpu.make_async_copy(y_hbm.at[pl.ds(row2, 1)], y2_ref.at[dst_slot, pl.ds(r, 1)], sem_y.at[dst_slot]))

    def start_tile(tile, dst_slot):
        for cp in block_copies(tile, dst_slot):
            cp.start()

        def body(r, carry):
            t = tile * tr + r
            for cp in y_copies(pos1_ref[t], pos2_ref[t], dst_slot, r):
                cp.start()
            return carry
        lax.fori_loop(0, tr, body, 0, unroll=8)

    def wait_tile(tile, dst_slot):
        for cp in block_copies(tile, dst_slot):
            cp.wait()

        def body(r, carry):
            for cp in y_copies(0, 0, dst_slot, r):
                cp.wait()
            return carry
        lax.fori_loop(0, tr, body, 0, unroll=8)

    @pl.when(i == 0)
    def _():
        start_tile(0, 0)

    @pl.when(i + 1 < n_tiles)
    def _():
        start_tile(i + 1, 1 - slot)

    wait_tile(i, slot)
    route = r_ref[slot]
    lane = lax.broadcasted_iota(jnp.int32, route.shape, 1).astype(F32)
    in_experts = lane < N_EXPERTS
    w1 = jnp.sum(jnp.where(jnp.logical_and(in_experts, lane == route[:, N_EXPERTS:N_EXPERTS + 1]), route, 0.0),
                 axis=-1, keepdims=True)
    w2 = jnp.sum(jnp.where(jnp.logical_and(in_experts, lane == route[:, N_EXPERTS + 1:N_EXPERTS + 2]), route, 0.0),
                 axis=-1, keepdims=True)
    x = h_ref[slot] + w1 * y1_ref[slot] + w2 * y2_ref[slot]
    ms = jnp.mean(x * x, axis=-1, keepdims=True)
    o_ref[...] = x * lax.rsqrt(ms + NORM_EPS) * w_ref[...]


def moe_combine(ys, h, route, final_norm, pos1, pos2, *, bsz, n_pad, n_frames, tr=MOE_ROW_TILE):
    d = h.shape[1]
    kernel = functools.partial(_combine_kernel, n_pad=n_pad, n_frames=n_frames)
    any_spec = pl.BlockSpec(memory_space=pl.ANY)
    return pl.pallas_call(
        kernel,
        grid_spec=pltpu.PrefetchScalarGridSpec(
            num_scalar_prefetch=2,
            grid=(bsz * n_frames // tr,),
            in_specs=[any_spec, any_spec, any_spec, pl.BlockSpec((1, d), lambda i, p1, p2: (0, 0))],
            out_specs=pl.BlockSpec((tr, d), lambda i, p1, p2: (i, 0)),
            scratch_shapes=[pltpu.VMEM((2, tr, d), F32), pltpu.VMEM((2, tr, d), F32), pltpu.VMEM((2, tr, d), F32),
                            pltpu.VMEM((2, tr, LANES), F32), pltpu.SemaphoreType.DMA((2,)),
                            pltpu.SemaphoreType.DMA((2, 2))]),
        out_shape=jax.ShapeDtypeStruct((bsz * n_frames, d), F32),
        compiler_params=_compiler_params(("arbitrary",)),
    )(pos1, pos2, ys, h, route, final_norm.reshape(1, d))


def _moe_plan(route, *, bsz, n_pad, n_frames, tm):
    n_tok = bsz * n_frames
    cap = 2 * n_tok + N_EXPERTS * tm
    frames = route.reshape(bsz, n_pad, LANES)[:, N_META:N_META + n_frames].reshape(n_tok, LANES)
    idx = frames[:, N_EXPERTS:N_EXPERTS + 2].astype(jnp.int32)
    sel = (idx[:, :, None] == jnp.arange(N_EXPERTS)[None, None, :]).any(axis=1)
    counts = sel.sum(axis=0)
    padded = ((counts + tm - 1) // tm) * tm
    ends = jnp.cumsum(padded)
    starts = ends - padded
    rank = jnp.cumsum(sel, axis=0) - 1
    pos = starts[None, :] + rank
    pos12 = jnp.take_along_axis(pos, idx, axis=1).astype(jnp.int32)
    tok_rows = (jnp.arange(n_tok) // n_frames) * n_pad + N_META + jnp.arange(n_tok) % n_frames
    tok = jnp.zeros((cap,), jnp.int32).at[pos12.reshape(-1)].set(
        jnp.repeat(tok_rows.astype(jnp.int32), 2), mode="drop")
    tile_start = jnp.arange(cap // tm) * tm
    tile_grp = jnp.minimum((tile_start[:, None] >= ends[None, :]).sum(axis=1), N_EXPERTS - 1).astype(jnp.int32)
    rows_used = ends[-1].astype(jnp.int32).reshape(1)
    return tok, tile_grp, rows_used, pos12[:, 0], pos12[:, 1]


def _dense_plan(m, tm, group):
    return jnp.asarray(np.full((m // tm,), group, np.int32)), jnp.asarray(np.full((1,), m // tm, np.int32))


def mixer_tables(n_pad):
    diag, col = attention_tables(n_pad)
    expand = np.arange(LANES)[:, None] == (np.arange(SSM_WIDTH) // SSM_HEAD_DIM)[None, :]
    return {"attn_diag": diag, "attn_col": col, "expand": jnp.asarray(expand, BF16)}


def mixer_param_stacks(mix_norm, conv_b, dt_bias, a_log, d_skip, ssm_norm, lambda_q1, lambda_k1, lambda_q2, lambda_k2,
                       diff_norm):
    n_layers, d = mix_norm.shape
    return {
        "mix_norm": mix_norm.reshape(n_layers, 1, d),
        "ssd": ssd_param_stacks(conv_b, dt_bias, a_log, d_skip, ssm_norm),
        "lam": jnp.stack([lambda_q1, lambda_k1, lambda_q2, lambda_k2], axis=1).astype(F32),
        "diff_norm": diff_norm.reshape(n_layers, 1, V_DIM),
    }


def mixer_layer(h, layer, p, *, bsz, n_pad, tm, hn=None):
    m, d = h.shape
    grp, nused = _dense_plan(m, tm, layer)
    mm = functools.partial(grouped_matmul, grp=grp, nused=nused, tm=tm)
    if hn is None:
        hn = rmsnorm(h, p["mix_norm"], layer)
    mm_in = functools.partial(mm, [hn], [p["mix_w_in_t"]], [0], [], w_transposed=True)
    qkv = mm_in(_epi_plain, ncols=QKV_COLS, tn=512, out_dtype=BF16)
    zx = mm_in(_epi_plain, ncols=ZX_COLS, tn=512, out_dtype=F32, w_col0=(ZX_COL0,))
    dtr = mm_in(_epi_plain, ncols=LANES, tn=LANES, out_dtype=F32, w_col0=(DT_COL0,))
    gates = mm_in(_epi_sigmoid, ncols=2 * d, tn=512, out_dtype=BF16, w_col0=(DT_COL0,),
                  w_shift=(GATE_COL0 - DT_COL0,))
    o_attn = diff_attention(qkv, p["lam"], p["diff_norm"], layer, p["attn_diag"], p["attn_col"], bsz=bsz,
                            n_pad=n_pad)
    o_ssm = ssd_mixer(zx, dtr, p["conv_w"], p["ssd"], layer, p["expand"], bsz=bsz, n_pad=n_pad)
    merged = mm([o_attn, o_ssm], [p["w_proj_attn"], p["w_proj_ssm"]], [0, 1], [gates, gates], _epi_gated_sum,
                ncols=d, tn=512, out_dtype=BF16, extra_col0=(0, d))
    return mm([merged], [p["w_out"]], [0], [h], _epi_residual, ncols=d, tn=512, out_dtype=F32, extra_col0=(0,))


def dense_ffn(h, norm_stack, layer, w_gate, w_up, w_down, group, *, tm):
    m, d = h.shape
    d_ff = w_gate.shape[2]
    grp, nused = _dense_plan(m, tm, group)
    mm = functools.partial(grouped_matmul, grp=grp, nused=nused, tm=tm)
    hn = rmsnorm(h, norm_stack, layer)
    u = mm([hn], [w_gate, w_up], [0, 0], [], _epi_swiglu, ncols=d_ff, tn=256, out_dtype=BF16)
    tm_down = tm // 2
    grp, nused = _dense_plan(m, tm_down, group)
    return grouped_matmul([u], [w_down.astype(BF16)], [0], [h], _epi_residual, grp=grp, nused=nused, ncols=d,
                          tm=tm_down, tn=512, out_dtype=F32, extra_col0=(0,))


def moe_ffn_final(h, norm_stack, layer, router_w, w_gate, w_up, w_down, final_norm, group, *, bsz, n_pad, n_frames):
    d = h.shape[1]
    d_fe = w_gate.shape[3]
    r_pad = jnp.pad(router_w.astype(F32), ((0, 0), (0, LANES - N_EXPERTS)))
    route = route_tokens(h, norm_stack, layer, r_pad)
    tok, tile_grp, rows_used, pos1, pos2 = _moe_plan(route, bsz=bsz, n_pad=n_pad, n_frames=n_frames,
                                                      tm=MOE_MM_ROW_TILE)
    xs = moe_dispatch(h, norm_stack, layer, tok, rows_used // MOE_ROW_TILE)
    mm = functools.partial(grouped_matmul, grp=tile_grp + group * N_EXPERTS, nused=rows_used // MOE_MM_ROW_TILE,
                           tm=MOE_MM_ROW_TILE)
    u = mm([xs], [w_gate.reshape(-1, d, d_fe), w_up.reshape(-1, d, d_fe)], [0, 0], [], _epi_swiglu,
           ncols=d_fe, tn=512, out_dtype=BF16)
    ys = mm([u], [w_down.reshape(-1, d_fe, d)], [0], [], _epi_plain, ncols=d, tn=1024, out_dtype=F32)
    return moe_combine(ys, h, route, final_norm, pos1, pos2, bsz=bsz, n_pad=n_pad, n_frames=n_frames)


def kernel(x, meta_tokens, mix_norm, mix_w_in, conv_w, conv_b, dt_bias, a_log, d_skip, ssm_norm, lambda_q1, lambda_k1, lambda_q2, lambda_k2, diff_norm, w_proj_attn, w_proj_ssm, w_out, ffn_norm, ffn_w_gate, ffn_w_up, ffn_w_down, router, moe_w_gate, moe_w_up, moe_w_down, final_norm):
    bsz, n_frames, d = x.shape
    assert mix_norm.shape[0] == 2 and d == D_MODEL, "layer 0: dense FFN; layer 1: MoE FFN, then the final norm"
    n = N_META + n_frames
    n_pad = -(-n // Q_BLOCK) * Q_BLOCK
    m = bsz * n_pad
    tm = n_pad // 2
    assert tm % 16 == 0 and n_pad % SSD_BLOCK == 0 and n_frames % MOE_ROW_TILE == 0

    p = dict(mix_w_in_t=jnp.swapaxes(mix_w_in, 1, 2), conv_w=conv_w, w_proj_attn=w_proj_attn, w_proj_ssm=w_proj_ssm,
             w_out=w_out)
    p.update(mixer_param_stacks(mix_norm, conv_b, dt_bias, a_log, d_skip, ssm_norm, lambda_q1, lambda_k1, lambda_q2,
                                lambda_k2, diff_norm))
    p.update(mixer_tables(n_pad))
    ffn_norm_stack = ffn_norm.reshape(ffn_norm.shape[0], 1, d)
    h, hn = embed_norm(x, meta_tokens, p["mix_norm"], 0)
    assert h.shape == (m, d)
    h = mixer_layer(h, 0, p, bsz=bsz, n_pad=n_pad, tm=tm, hn=hn)
    h = dense_ffn(h, ffn_norm_stack, 0, ffn_w_gate, ffn_w_up, ffn_w_down, 0, tm=tm)
    h = mixer_layer(h, 1, p, bsz=bsz, n_pad=n_pad, tm=tm)
    out = moe_ffn_final(h, ffn_norm_stack, 1, router[0], moe_w_gate, moe_w_up, moe_w_down, final_norm, 0,
                        bsz=bsz, n_pad=n_pad, n_frames=n_frames)
    return out.reshape(bsz, n_frames, d)
```

```python
import functools
import math

import jax
import jax.numpy as jnp
import numpy as np
from jax import lax
from jax.experimental import pallas as pl
from jax.experimental.pallas import tpu as pltpu

F32 = jnp.float32
BF16 = jnp.bfloat16

LANES = 128
VMEM_LIMIT_BYTES_V7X = 56 * 2**20

D_MODEL = 4096
N_META = 16
CHUNK = 64
Q_BLOCK = 128
ATTN_ROWS = 2 * Q_BLOCK
HEADS = 8
QK_DIM = 128
V_DIM = 256
ATTN_WIDTH = HEADS * V_DIM
SUBLN_EPS = 1e-5
SSM_WIDTH = 2048
SSM_HEAD_DIM = 64
SSM_HEADS = 32
SSM_GROUPS = 4
SSM_STATE = 128
SSM_CONV = 4
SSM_GROUP_WIDTH = SSM_WIDTH // SSM_GROUPS
XBC_WIDTH = SSM_WIDTH + 2 * SSM_GROUPS * SSM_STATE
QKV_COLS = 3 * ATTN_WIDTH
ZX_COL0 = QKV_COLS
ZX_COLS = SSM_WIDTH + XBC_WIDTH
DT_COL0 = ZX_COL0 + ZX_COLS
GATE_COL0 = DT_COL0 + SSM_HEADS
N_EXPERTS = 8
NORM_EPS = 1e-6
NEG_BIG = -1e30
LOG2_E = 1.4426950408889634
SSD_BLOCK = 128

MOE_ROW_TILE = 256
MOE_MM_ROW_TILE = 512
NORM_ROW_TILE = 272


def _sigmoid(x):
    return 0.5 * jnp.tanh(0.5 * x) + 0.5


def _silu(x):
    return x * _sigmoid(x)


def _compiler_params(semantics):
    return pltpu.CompilerParams(dimension_semantics=semantics, vmem_limit_bytes=VMEM_LIMIT_BYTES_V7X)


def _rmsnorm_kernel(x_ref, w_ref, o_ref, *, eps):
    x = x_ref[...]
    ms = jnp.mean(x * x, axis=-1, keepdims=True)
    o_ref[...] = (x * lax.rsqrt(ms + eps) * w_ref[...]).astype(o_ref.dtype)


def _norm_row_tile(m):
    return NORM_ROW_TILE if m % NORM_ROW_TILE == 0 else Q_BLOCK


def _layer_spec(rows, width, layer):
    return pl.BlockSpec((None, rows, width), lambda *_: (layer, 0, 0))


def rmsnorm(x, w_stack, layer, *, eps=NORM_EPS, out_dtype=BF16):
    m, d = x.shape
    tr = _norm_row_tile(m)
    return pl.pallas_call(
        functools.partial(_rmsnorm_kernel, eps=eps),
        grid=(m // tr,),
        in_specs=[pl.BlockSpec((tr, d), lambda i: (i, 0)), _layer_spec(1, d, layer)],
        out_specs=pl.BlockSpec((tr, d), lambda i: (i, 0)),
        out_shape=jax.ShapeDtypeStruct((m, d), out_dtype),
        compiler_params=_compiler_params(("parallel",)),
    )(x, w_stack)


def _embed_norm_kernel(x_ref, meta_ref, w_ref, h_ref, hn_ref, *, eps):
    c = pl.program_id(1)
    last = pl.num_programs(1) - 1
    tail = Q_BLOCK - N_META

    def emit(rows):
        h_ref[...] = rows
        ms = jnp.mean(rows * rows, axis=-1, keepdims=True)
        hn_ref[...] = (rows * lax.rsqrt(ms + eps) * w_ref[...]).astype(hn_ref.dtype)

    @pl.when(c == 0)
    def _():
        emit(jnp.concatenate([meta_ref[...], x_ref[0, 0:tail, :]], axis=0))

    @pl.when(jnp.logical_and(c > 0, c < last))
    def _():
        emit(x_ref[0])

    @pl.when(c == last)
    def _():
        emit(jnp.concatenate([x_ref[0, tail:Q_BLOCK, :], jnp.zeros((tail, x_ref.shape[2]), F32)], axis=0))


def embed_norm(x, meta_tokens, w_stack, layer):
    bsz, n_frames, d = x.shape
    assert n_frames % Q_BLOCK == 0 and meta_tokens.shape == (N_META, d)
    nc = n_frames // Q_BLOCK + 1
    m = bsz * nc * Q_BLOCK

    def x_index(b, c):
        start = jnp.clip(c * Q_BLOCK - N_META, 0, n_frames - Q_BLOCK)
        return b, pl.multiple_of(start, N_META), 0

    row_spec = pl.BlockSpec((Q_BLOCK, d), lambda b, c: (b * nc + c, 0))
    return pl.pallas_call(
        functools.partial(_embed_norm_kernel, eps=NORM_EPS),
        grid=(bsz, nc),
        in_specs=[pl.BlockSpec((pl.Element(1), pl.Element(Q_BLOCK), pl.Element(d)), x_index),
                  pl.BlockSpec((N_META, d), lambda b, c: (0, 0)), _layer_spec(1, d, layer)],
        out_specs=[row_spec, row_spec],
        out_shape=[jax.ShapeDtypeStruct((m, d), F32), jax.ShapeDtypeStruct((m, d), BF16)],
        compiler_params=_compiler_params(("parallel", "arbitrary")),
    )(x, meta_tokens.astype(F32), w_stack)


def _split_bf16(x):
    hi = x.astype(BF16)
    lo = (x - hi.astype(F32)).astype(BF16)
    return hi, lo


def _router_kernel(x_ref, w_ref, r_ref, o_ref, *, eps):
    x = x_ref[...]
    ms = jnp.mean(x * x, axis=-1, keepdims=True)
    hn = x * lax.rsqrt(ms + eps) * w_ref[...]
    h_hi, h_lo = _split_bf16(hn)
    r_hi, r_lo = _split_bf16(r_ref[...])
    logits = (jnp.dot(h_hi, r_hi, preferred_element_type=F32)
              + jnp.dot(h_hi, r_lo, preferred_element_type=F32)
              + jnp.dot(h_lo, r_hi, preferred_element_type=F32))
    lane = lax.broadcasted_iota(jnp.int32, logits.shape, 1).astype(F32)
    lg = jnp.where(lane < N_EXPERTS, logits, -jnp.inf)
    m1 = jnp.max(lg, axis=-1, keepdims=True)
    i1 = jnp.min(jnp.where(lg == m1, lane, float(LANES)), axis=-1, keepdims=True)
    lg2 = jnp.where(lane == i1, -jnp.inf, lg)
    m2 = jnp.max(lg2, axis=-1, keepdims=True)
    i2 = jnp.min(jnp.where(lg2 == m2, lane, float(LANES)), axis=-1, keepdims=True)
    e2 = jnp.exp(m2 - m1)
    den = 1.0 + e2
    out = jnp.where(lane == i1, 1.0 / den, 0.0) + jnp.where(lane == i2, e2 / den, 0.0)
    out = jnp.where(lane == N_EXPERTS, i1, out)
    out = jnp.where(lane == N_EXPERTS + 1, i2, out)
    o_ref[...] = out


def route_tokens(x, w_stack, layer, r_pad):
    m, d = x.shape
    tr = _norm_row_tile(m)
    return pl.pallas_call(
        functools.partial(_router_kernel, eps=NORM_EPS),
        grid=(m // tr,),
        in_specs=[pl.BlockSpec((tr, d), lambda i: (i, 0)), _layer_spec(1, d, layer),
                  pl.BlockSpec((d, LANES), lambda i: (0, 0))],
        out_specs=pl.BlockSpec((tr, LANES), lambda i: (i, 0)),
        out_shape=jax.ShapeDtypeStruct((m, LANES), F32),
        compiler_params=_compiler_params(("parallel",)),
    )(x, w_stack, r_pad)


def _mm_kernel(grp_ref, nused_ref, *refs, n_a, w_to_a, cast, w_shift, w_transposed, n_extra, nk, epilogue):
    col_axis = 0 if w_transposed else 1
    n_w = len(w_to_a)
    a_refs = refs[:n_a]
    w_refs = refs[n_a:n_a + n_w]
    refs = list(refs[n_a + n_w:])
    wnext_refs = [refs.pop(0) if s else None for s in w_shift]
    e_refs = [refs.pop(0) for _ in range(n_extra)]
    o_ref = refs.pop(0)
    scratch = refs
    wb_refs = [scratch.pop(0) if c else None for c in cast]
    acc_refs = [scratch.pop(0) for _ in range(n_w)] if nk > 1 else []
    i = pl.program_id(1)
    k = pl.program_id(2)

    @pl.when(i >= nused_ref[0])
    def _():
        o_ref[...] = jnp.zeros_like(o_ref)

    @pl.when(i < nused_ref[0])
    def _():
        if any(cast):
            def do_cast():
                for w_ref, wn_ref, wb_ref, s in zip(w_refs, wnext_refs, wb_refs, w_shift):
                    if wb_ref is None:
                        continue
                    if s:
                        tn = w_ref.shape[col_axis]
                        wide = jnp.concatenate([w_ref[...], wn_ref[...]], axis=col_axis)
                        wide = wide[s:s + tn, :] if w_transposed else wide[:, s:s + tn]
                        wb_ref[...] = wide.astype(BF16)
                    else:
                        wb_ref[...] = w_ref[...].astype(BF16)
            if nk == 1:
                changed = jnp.logical_or(i == 0, grp_ref[i] != grp_ref[jnp.maximum(i - 1, 0)])
                pl.when(changed)(do_cast)
            else:
                do_cast()
        parts = []
        for wi in range(n_w):
            w = wb_refs[wi][...] if cast[wi] else w_refs[wi][...]
            contract = (((1,), (1,)), ((), ())) if w_transposed else (((1,), (0,)), ((), ()))
            parts.append(lax.dot_general(a_refs[w_to_a[wi]][...], w, contract, preferred_element_type=F32))
        if nk == 1:
            o_ref[...] = epilogue(parts, e_refs).astype(o_ref.dtype)
        else:
            @pl.when(k == 0)
            def _():
                for acc, p in zip(acc_refs, parts):
                    acc[...] = p

            @pl.when(k > 0)
            def _():
                for acc, p in zip(acc_refs, parts):
                    acc[...] += p

            @pl.when(k == nk - 1)
            def _():
                o_ref[...] = epilogue([acc[...] for acc in acc_refs], e_refs).astype(o_ref.dtype)


def grouped_matmul(a_list, w_list, w_to_a, extras, epilogue, *, grp, nused, ncols, tm, tn, out_dtype,
                   w_col0=None, w_shift=None, w_transposed=False, extra_col0=(), tk=None):
    m, kdim = a_list[0].shape
    tk = kdim if tk is None else tk
    nk = kdim // tk
    w_col0 = (0,) * len(w_list) if w_col0 is None else tuple(w_col0)
    w_shift = (0,) * len(w_list) if w_shift is None else tuple(w_shift)
    assert m % tm == 0 and ncols % tn == 0 and kdim % tk == 0 and all(c % tn == 0 for c in w_col0)
    assert all(a.shape == (m, kdim) for a in a_list)
    assert all(w.shape[2 if w_transposed else 1] == kdim for w in w_list)
    cast = tuple(w.dtype != BF16 for w in w_list)
    assert all(c or not s for c, s in zip(cast, w_shift)) and all(0 <= s < LANES for s in w_shift)
    assert not w_transposed or all(s % 8 == 0 for s in w_shift)
    ecb = tuple(c // tn for c in extra_col0)
    lane_tiles = tn // LANES

    def row(i, n):
        return jnp.minimum(i, n[0] - 1)

    def w_block(width, col_index):
        if w_transposed:
            return pl.BlockSpec((None, width, tk), lambda j, i, k, g, n: (g[row(i, n)], col_index(j), k))
        return pl.BlockSpec((None, tk, width), lambda j, i, k, g, n: (g[row(i, n)], k, col_index(j)))

    def w_spec(cb):
        return w_block(tn, lambda j: cb + j)

    def w_next_spec(cb):
        return w_block(LANES, lambda j: (cb + j + 1) * lane_tiles)

    in_specs = [pl.BlockSpec((tm, tk), lambda j, i, k, g, n: (row(i, n), k)) for _ in a_list]
    in_specs += [w_spec(c // tn) for c in w_col0]
    in_specs += [w_next_spec(c // tn) for c, s in zip(w_col0, w_shift) if s]
    in_specs += [pl.BlockSpec((tm, tn), functools.partial(lambda j, i, k, g, n, eb: (row(i, n), eb + j), eb=eb))
                 for eb in ecb]
    scratch = [pltpu.VMEM((tn, tk) if w_transposed else (tk, tn), BF16) for c in cast if c]
    if nk > 1:
        scratch += [pltpu.VMEM((tm, tn), F32) for _ in w_list]
    kernel = functools.partial(_mm_kernel, n_a=len(a_list), w_to_a=tuple(w_to_a), cast=cast, w_shift=w_shift,
                               w_transposed=w_transposed, n_extra=len(extras), nk=nk, epilogue=epilogue)
    w_next = [w for w, s in zip(w_list, w_shift) if s]
    return pl.pallas_call(
        kernel,
        grid_spec=pltpu.PrefetchScalarGridSpec(
            num_scalar_prefetch=2,
            grid=(ncols // tn, m // tm, nk),
            in_specs=in_specs,
            out_specs=pl.BlockSpec((tm, tn), lambda j, i, k, g, n: (i, j)),
            scratch_shapes=scratch),
        out_shape=jax.ShapeDtypeStruct((m, ncols), out_dtype),
        compiler_params=_compiler_params(("arbitrary", "arbitrary", "arbitrary")),
    )(grp, nused, *a_list, *w_list, *w_next, *extras)


def _epi_plain(parts, e_refs):
    return parts[0]


def _epi_sigmoid(parts, e_refs):
    return _sigmoid(parts[0])


def _epi_gated_sum(parts, e_refs):
    return e_refs[0][...] * parts[0] + e_refs[1][...] * parts[1]


def _epi_residual(parts, e_refs):
    return e_refs[0][...] + parts[0]


def _epi_swiglu(parts, e_refs):
    return _silu(parts[0]) * parts[1]


def attention_tables(n_pad):
    slopes = np.exp2(-8.0 * (np.arange(HEADS, dtype=np.float64) + 1.0) / HEADS) * LOG2_E
    i = np.arange(ATTN_ROWS)[:, None]
    j = np.arange(ATTN_ROWS + Q_BLOCK)[None, :]
    visible = (i - N_META) // CHUNK >= (j - N_META) // CHUNK
    rel = (i - np.abs(i - j)).astype(np.float64)
    diag = np.where(visible[None], slopes[:, None, None] * rel[None], NEG_BIG)
    col = slopes[:, None, None] * np.arange(n_pad, dtype=np.float64)[None, None, :]
    return jnp.asarray(diag, F32), jnp.asarray(col, F32)


def _attn_kernel(q_ref, k_ref, v_ref, lam_ref, dn_ref, diag_ref, col_ref, o_ref, kt_ref, *, lambda_init):
    n_pad = q_ref.shape[0]
    lp = lam_ref[...]
    s1 = jnp.sum(lp[0:1] * lp[1:2], axis=-1, keepdims=True)
    s2 = jnp.sum(lp[2:3] * lp[3:4], axis=-1, keepdims=True)
    lam = jnp.exp(s1) - jnp.exp(s2) + lambda_init
    for mp in range(2):
        kt_ref[mp] = k_ref[:, mp * QK_DIM:(mp + 1) * QK_DIM].T
    scale2 = QK_DIM ** -0.5 * LOG2_E
    gain = dn_ref[...] * (1.0 - lambda_init)
    for q0 in range(0, n_pad, ATTN_ROWS):
        qb = q0 // ATTN_ROWS
        n_rows = min(ATTN_ROWS, n_pad - q0)
        width = min(n_rows + Q_BLOCK, n_pad - q0)
        rows = slice(q0, q0 + n_rows)
        diag_bias = diag_ref[:n_rows, :width]
        past_bias = col_ref[:, :q0] - col_ref[:, q0:q0 + 1] if qb else None
        heads = []
        for mp in range(2):
            q = q_ref[rows, mp * QK_DIM:(mp + 1) * QK_DIM]
            s_d = jnp.dot(q, kt_ref[mp, :, q0:q0 + width], preferred_element_type=F32) * scale2 + diag_bias
            mx = jnp.max(s_d, axis=-1, keepdims=True)
            if qb:
                s_p = jnp.dot(q, kt_ref[mp, :, :q0], preferred_element_type=F32) * scale2 + past_bias
                mx = jnp.maximum(mx, jnp.max(s_p, axis=-1, keepdims=True))
            e_d = jnp.exp2(s_d - mx)
            den = jnp.sum(e_d, axis=-1, keepdims=True)
            pv = jnp.dot(e_d.astype(BF16), v_ref[q0:q0 + width, :], preferred_element_type=F32)
            if qb:
                e_p = jnp.exp2(s_p - mx)
                den = den + jnp.sum(e_p, axis=-1, keepdims=True)
                pv = pv + jnp.dot(e_p.astype(BF16), v_ref[:q0, :], preferred_element_type=F32)
            heads.append(pv / den)
        o = heads[0] - lam * heads[1]
        ms = jnp.mean(o * o, axis=-1, keepdims=True)
        o_ref[rows, :] = (o * lax.rsqrt(ms + SUBLN_EPS) * gain).astype(o_ref.dtype)


def diff_attention(qkv, lam_stack, diff_norm_stack, layer, diag, col, *, bsz, n_pad):
    m = qkv.shape[0]
    kernel = functools.partial(_attn_kernel, lambda_init=0.8 - 0.6 * math.exp(-0.3 * layer))
    return pl.pallas_call(
        kernel,
        grid=(bsz, HEADS),
        in_specs=[
            pl.BlockSpec((n_pad, V_DIM), lambda b, h: (b, h)),
            pl.BlockSpec((n_pad, V_DIM), lambda b, h: (b, HEADS + h)),
            pl.BlockSpec((n_pad, V_DIM), lambda b, h: (b, 2 * HEADS + h)),
            _layer_spec(4, QK_DIM, layer),
            _layer_spec(1, V_DIM, layer),
            pl.BlockSpec((None, ATTN_ROWS, ATTN_ROWS + Q_BLOCK), lambda b, h: (h, 0, 0)),
            pl.BlockSpec((None, 1, n_pad), lambda b, h: (h, 0, 0)),
        ],
        out_specs=pl.BlockSpec((n_pad, V_DIM), lambda b, h: (b, h)),
        out_shape=jax.ShapeDtypeStruct((m, ATTN_WIDTH), BF16),
        scratch_shapes=[pltpu.VMEM((2, QK_DIM, n_pad), BF16)],
        compiler_params=_compiler_params(("parallel", "parallel")),
    )(qkv, qkv, qkv, lam_stack, diff_norm_stack, diag, col)


def _ssd_kernel(zx_ref, dtr_ref, cw_ref, cb_ref, dtb_ref, alog_ref, dsk_ref, nw_ref, ex_ref, o_ref,
                state_ref, xbuf_ref, y_ref):
    c = pl.program_id(1)
    L = SSD_BLOCK
    halo = 8

    @pl.when(c == 0)
    def _():
        state_ref[...] = jnp.zeros_like(state_ref)
        xbuf_ref[0:halo, :] = jnp.zeros((halo, XBC_WIDTH), F32)

    @pl.when(c > 0)
    def _():
        xbuf_ref[0:halo, :] = xbuf_ref[L:L + halo, :]

    u = zx_ref[:, SSM_WIDTH:SSM_WIDTH + XBC_WIDTH]
    xbuf_ref[halo:halo + L, :] = u
    acc = cb_ref[...] + cw_ref[SSM_CONV - 1:SSM_CONV, :] * u
    for s in range(1, SSM_CONV):
        acc = acc + cw_ref[SSM_CONV - 1 - s:SSM_CONV - s, :] * xbuf_ref[halo - s:halo - s + L, :]
    xbc = _silu(acc)
    xs = xbc[:, :SSM_WIDTH]
    bm = xbc[:, SSM_WIDTH:SSM_WIDTH + SSM_GROUPS * SSM_STATE].astype(BF16)
    cm = xbc[:, SSM_WIDTH + SSM_GROUPS * SSM_STATE:].astype(BF16)

    pre = dtr_ref[...] + dtb_ref[...]
    dt = jnp.maximum(pre, 0.0) + jnp.log(1.0 + jnp.exp(-jnp.abs(pre)))
    adt = dt * (-jnp.exp(alog_ref[...]))
    rows = lax.broadcasted_iota(jnp.int32, (L, LANES), 0)
    cum = adt
    shift = 1
    while shift < L:
        cum = cum + jnp.where(rows >= shift, pltpu.roll(cum, shift, 0), 0.0)
        shift *= 2
    cum_last = cum[L - 1:L, :]
    ecum = jnp.exp(cum)
    edec = jnp.exp(cum_last - cum)
    elast = jnp.broadcast_to(jnp.exp(cum_last), (8, LANES))
    stacked = jnp.concatenate([dt, ecum, edec, elast], axis=0)
    ex = ex_ref[...]
    wide = sum(jnp.dot(p, ex, preferred_element_type=F32) for p in _split_bf16(stacked))
    dt_w, ecum_w, edec_w, elast_w = wide[0:L], wide[L:2 * L], wide[2 * L:3 * L], wide[3 * L:3 * L + 1]

    xd = xs * dt_w
    xd_b = xd.astype(BF16)
    xdec_b = (xd * edec_w).astype(BF16)
    cum_t = cum.T
    tril = lax.broadcasted_iota(jnp.int32, (L, L), 0) >= lax.broadcasted_iota(jnp.int32, (L, L), 1)
    heads_per_group = SSM_HEADS // SSM_GROUPS
    for g in range(SSM_GROUPS):
        gs = slice(g * SSM_GROUP_WIDTH, (g + 1) * SSM_GROUP_WIDTH)
        bg = bm[:, g * SSM_STATE:(g + 1) * SSM_STATE]
        cg = cm[:, g * SSM_STATE:(g + 1) * SSM_STATE]
        cb = lax.dot_general(cg, bg, (((1,), (1,)), ((), ())), preferred_element_type=F32)
        st = state_ref[:, gs]
        y_off = jnp.dot(cg, st.astype(BF16), preferred_element_type=F32) * ecum_w[:, gs]
        state_ref[:, gs] = elast_w[:, gs] * st + lax.dot_general(
            bg, xdec_b[:, gs], (((0,), (0,)), ((), ())), preferred_element_type=F32)
        y_ref[:, gs] = y_off
        for r in range(heads_per_group):
            hd = g * heads_per_group + r
            hs = slice(hd * SSM_HEAD_DIM, (hd + 1) * SSM_HEAD_DIM)
            seg = cum[:, hd:hd + 1] - cum_t[hd:hd + 1, :]
            lmat = jnp.exp(jnp.where(tril, seg, -jnp.inf))
            y_ref[:, hs] += jnp.dot((cb * lmat).astype(BF16), xd_b[:, hs], preferred_element_type=F32)

    y = y_ref[...] + dsk_ref[...] * xs
    gated = y * _silu(zx_ref[:, 0:SSM_WIDTH])
    for g in range(SSM_GROUPS):
        gs = slice(g * SSM_GROUP_WIDTH, (g + 1) * SSM_GROUP_WIDTH)
        gg = gated[:, gs]
        ms = jnp.mean(gg * gg, axis=-1, keepdims=True)
        o_ref[:, gs] = (gg * lax.rsqrt(ms + NORM_EPS) * nw_ref[:, gs]).astype(o_ref.dtype)


def ssd_param_stacks(conv_b, dt_bias, a_log, d_skip, ssm_norm):
    n_layers = conv_b.shape[0]

    def pad_heads(v):
        return jnp.pad(v.astype(F32), ((0, 0), (0, LANES - SSM_HEADS))).reshape(n_layers, 1, LANES)

    return (conv_b.reshape(n_layers, 1, XBC_WIDTH), pad_heads(dt_bias), pad_heads(a_log),
            jnp.repeat(d_skip.astype(F32), SSM_HEAD_DIM, axis=1).reshape(n_layers, 1, SSM_WIDTH),
            ssm_norm.reshape(n_layers, 1, SSM_WIDTH))


def ssd_mixer(zx, dtr, conv_w, stacks, layer, expand, *, bsz, n_pad):
    m = zx.shape[0]
    nc = n_pad // SSD_BLOCK
    L = SSD_BLOCK
    return pl.pallas_call(
        _ssd_kernel,
        grid=(bsz, nc),
        in_specs=[
            pl.BlockSpec((L, ZX_COLS), lambda b, c: (b * nc + c, 0)),
            pl.BlockSpec((L, LANES), lambda b, c: (b * nc + c, 0)),
            _layer_spec(SSM_CONV, XBC_WIDTH, layer), _layer_spec(1, XBC_WIDTH, layer), _layer_spec(1, LANES, layer),
            _layer_spec(1, LANES, layer), _layer_spec(1, SSM_WIDTH, layer), _layer_spec(1, SSM_WIDTH, layer),
            pl.BlockSpec((LANES, SSM_WIDTH), lambda b, c: (0, 0)),
        ],
        out_specs=pl.BlockSpec((L, SSM_WIDTH), lambda b, c: (b * nc + c, 0)),
        out_shape=jax.ShapeDtypeStruct((m, SSM_WIDTH), BF16),
        scratch_shapes=[pltpu.VMEM((SSM_STATE, SSM_WIDTH), F32), pltpu.VMEM((L + 8, XBC_WIDTH), F32),
                        pltpu.VMEM((L, SSM_WIDTH), F32)],
        compiler_params=_compiler_params(("parallel", "arbitrary")),
    )(zx, dtr, conv_w, *stacks, expand)


def _dispatch_kernel(tok_ref, nused_ref, h_hbm, w_ref, o_ref, buf_ref, sem):
    i = pl.program_id(0)
    g = buf_ref.shape[1]
    nused = nused_ref[0]
    slot = i % 2

    def row_copy(src_row, dst_slot, r):
        return pltpu.make_async_copy(h_hbm.at[pl.ds(src_row, 1)], buf_ref.at[dst_slot, pl.ds(r, 1)],
                                     sem.at[dst_slot])

    def start_tile(tile, dst_slot):
        def body(r2, carry):
            for priority in range(2):
                r = 2 * r2 + priority
                row_copy(tok_ref[tile * g + r], dst_slot, r).start(priority=priority)
            return carry
        lax.fori_loop(0, g // 2, body, 0, unroll=4)

    def wait_tile(dst_slot):
        def body(r, carry):
            row_copy(0, dst_slot, r).wait()
            return carry
        lax.fori_loop(0, g, body, 0, unroll=8)

    @pl.when(jnp.logical_and(i == 0, nused > 0))
    def _():
        start_tile(0, 0)

    @pl.when(i + 1 < nused)
    def _():
        start_tile(i + 1, 1 - slot)

    @pl.when(i < nused)
    def _():
        wait_tile(slot)
        x = buf_ref[slot]
        ms = jnp.mean(x * x, axis=-1, keepdims=True)
        o_ref[...] = (buf_ref[slot] * lax.rsqrt(ms + NORM_EPS) * w_ref[...]).astype(o_ref.dtype)

    @pl.when(i >= nused)
    def _():
        o_ref[...] = jnp.zeros_like(o_ref)


def moe_dispatch(h, w_stack, layer, tok, nused, *, tg=MOE_ROW_TILE):
    d = h.shape[1]
    rows = tok.shape[0]
    return pl.pallas_call(
        _dispatch_kernel,
        grid_spec=pltpu.PrefetchScalarGridSpec(
            num_scalar_prefetch=2,
            grid=(rows // tg,),
            in_specs=[pl.BlockSpec(memory_space=pl.ANY), _layer_spec(1, d, layer)],
            out_specs=pl.BlockSpec((tg, d), lambda i, t, n: (i, 0)),
            scratch_shapes=[pltpu.VMEM((2, tg, d), F32), pltpu.SemaphoreType.DMA((2,))]),
        out_shape=jax.ShapeDtypeStruct((rows, d), BF16),
        compiler_params=_compiler_params(("arbitrary",)),
    )(tok, nused, h, w_stack)


def _combine_kernel(pos1_ref, pos2_ref, y_hbm, h_hbm, route_hbm, w_ref, o_ref,
                    y1_ref, y2_ref, h_ref, r_ref, sem_y, sem_h, *, n_pad, n_frames):
    i = pl.program_id(0)
    n_tiles = pl.num_programs(0)
    tr = h_ref.shape[1]
    tiles_per_batch = n_frames // tr
    slot = i % 2

    def block_copies(tile, dst_slot):
        b = tile // tiles_per_batch
        row0 = b * n_pad + N_META + (tile - b * tiles_per_batch) * tr
        return (pltpu.make_async_copy(h_hbm.at[pl.ds(row0, tr)], h_ref.at[dst_slot], sem_h.at[0, dst_slot]),
                pltpu.make_async_copy(route_hbm.at[pl.ds(row0, tr)], r_ref.at[dst_slot], sem_h.at[1, dst_slot]))

    def y_copies(row1, row2, dst_slot, r):
        return (pltpu.make_async_copy(y_hbm.at[pl.ds(row1, 1)], y1_ref.at[dst_slot, pl.ds(r, 1)], sem_y.at[dst_slot]),
                pltpu.make_async_copy(y_hbm.at[pl.ds(row2, 1)], y2_ref.at[dst_slot, pl.ds(r, 1)], sem_y.at[dst_slot]))

    def start_tile(tile, dst_slot):
        for cp in block_copies(tile, dst_slot):
            cp.start()

        def body(r, carry):
            t = tile * tr + r
            for priority, cp in enumerate(y_copies(pos1_ref[t], pos2_ref[t], dst_slot, r)):
                cp.start(priority=priority)
            return carry
        lax.fori_loop(0, tr, body, 0, unroll=8)

    def wait_tile(tile, dst_slot):
        for cp in block_copies(tile, dst_slot):
            cp.wait()

        def body(r, carry):
            for cp in y_copies(0, 0, dst_slot, r):
                cp.wait()
            return carry
        lax.fori_loop(0, tr, body, 0, unroll=8)

    @pl.when(i == 0)
    def _():
        start_tile(0, 0)

    @pl.when(i + 1 < n_tiles)
    def _():
        start_tile(i + 1, 1 - slot)

    wait_tile(i, slot)
    route = r_ref[slot]
    lane = lax.broadcasted_iota(jnp.int32, route.shape, 1).astype(F32)
    in_experts = lane < N_EXPERTS
    w1 = jnp.sum(jnp.where(jnp.logical_and(in_experts, lane == route[:, N_EXPERTS:N_EXPERTS + 1]), route, 0.0),
                 axis=-1, keepdims=True)
    w2 = jnp.sum(jnp.where(jnp.logical_and(in_experts, lane == route[:, N_EXPERTS + 1:N_EXPERTS + 2]), route, 0.0),
                 axis=-1, keepdims=True)
    x = h_ref[slot] + w1 * y1_ref[slot] + w2 * y2_ref[slot]
    ms = jnp.mean(x * x, axis=-1, keepdims=True)
    o_ref[...] = x * lax.rsqrt(ms + NORM_EPS) * w_ref[...]


def moe_combine(ys, h, route, final_norm, pos1, pos2, *, bsz, n_pad, n_frames, tr=MOE_ROW_TILE):
    d = h.shape[1]
    kernel = functools.partial(_combine_kernel, n_pad=n_pad, n_frames=n_frames)
    any_spec = pl.BlockSpec(memory_space=pl.ANY)
    return pl.pallas_call(
        kernel,
        grid_spec=pltpu.PrefetchScalarGridSpec(
            num_scalar_prefetch=2,
            grid=(bsz * n_frames // tr,),
            in_specs=[any_spec, any_spec, any_spec, pl.BlockSpec((1, d), lambda i, p1, p2: (0, 0))],
            out_specs=pl.BlockSpec((tr, d), lambda i, p1, p2: (i, 0)),
            scratch_shapes=[pltpu.VMEM((2, tr, d), F32), pltpu.VMEM((2, tr, d), F32), pltpu.VMEM((2, tr, d), F32),
                            pltpu.VMEM((2, tr, LANES), F32), pltpu.SemaphoreType.DMA((2,)),
                            pltpu.SemaphoreType.DMA((2, 2))]),
        out_shape=jax.ShapeDtypeStruct((bsz * n_frames, d), F32),
        compiler_params=_compiler_params(("arbitrary",)),
    )(pos1, pos2, ys, h, route, final_norm.reshape(1, d))


def _moe_plan(route, *, bsz, n_pad, n_frames, tm):
    n_tok = bsz * n_frames
    cap = 2 * n_tok + N_EXPERTS * tm
    frames = route.reshape(bsz, n_pad, LANES)[:, N_META:N_META + n_frames].reshape(n_tok, LANES)
    idx = frames[:, N_EXPERTS:N_EXPERTS + 2].astype(jnp.int32)
    sel = (idx[:, :, None] == jnp.arange(N_EXPERTS)[None, None, :]).any(axis=1)
    counts = sel.sum(axis=0)
    padded = ((counts + tm - 1) // tm) * tm
    ends = jnp.cumsum(padded)
    starts = ends - padded
    rank = jnp.cumsum(sel, axis=0) - 1
    pos = starts[None, :] + rank
    pos12 = jnp.take_along_axis(pos, idx, axis=1).astype(jnp.int32)
    tok_rows = (jnp.arange(n_tok) // n_frames) * n_pad + N_META + jnp.arange(n_tok) % n_frames
    tok = jnp.zeros((cap,), jnp.int32).at[pos12.reshape(-1)].set(
        jnp.repeat(tok_rows.astype(jnp.int32), 2), mode="drop")
    tile_start = jnp.arange(cap // tm) * tm
    tile_grp = jnp.minimum((tile_start[:, None] >= ends[None, :]).sum(axis=1), N_EXPERTS - 1).astype(jnp.int32)
    rows_used = ends[-1].astype(jnp.int32).reshape(1)
    return tok, tile_grp, rows_used, pos12[:, 0], pos12[:, 1]


def _dense_plan(m, tm, group):
    return jnp.asarray(np.full((m // tm,), group, np.int32)), jnp.asarray(np.full((1,), m // tm, np.int32))


def mixer_tables(n_pad):
    diag, col = attention_tables(n_pad)
    expand = np.arange(LANES)[:, None] == (np.arange(SSM_WIDTH) // SSM_HEAD_DIM)[None, :]
    return {"attn_diag": diag, "attn_col": col, "expand": jnp.asarray(expand, BF16)}


def mixer_param_stacks(mix_norm, conv_b, dt_bias, a_log, d_skip, ssm_norm, lambda_q1, lambda_k1, lambda_q2, lambda_k2,
                       diff_norm):
    n_layers, d = mix_norm.shape
    return {
        "mix_norm": mix_norm.reshape(n_layers, 1, d),
        "ssd": ssd_param_stacks(conv_b, dt_bias, a_log, d_skip, ssm_norm),
        "lam": jnp.stack([lambda_q1, lambda_k1, lambda_q2, lambda_k2], axis=1).astype(F32),
        "diff_norm": diff_norm.reshape(n_layers, 1, V_DIM),
    }


def mixer_layer(h, layer, p, *, bsz, n_pad, tm, hn=None):
    m, d = h.shape
    grp, nused = _dense_plan(m, tm, layer)
    mm = functools.partial(grouped_matmul, grp=grp, nused=nused, tm=tm)
    if hn is None:
        hn = rmsnorm(h, p["mix_norm"], layer)
    mm_in = functools.partial(mm, [hn], [p["mix_w_in_t"]], [0], [], w_transposed=True)
    qkv = mm_in(_epi_plain, ncols=QKV_COLS, tn=512, out_dtype=BF16)
    zx = mm_in(_epi_plain, ncols=ZX_COLS, tn=512, out_dtype=F32, w_col0=(ZX_COL0,))
    dtr = mm_in(_epi_plain, ncols=LANES, tn=LANES, out_dtype=F32, w_col0=(DT_COL0,))
    gates = mm_in(_epi_sigmoid, ncols=2 * d, tn=512, out_dtype=BF16, w_col0=(DT_COL0,),
                  w_shift=(GATE_COL0 - DT_COL0,))
    o_attn = diff_attention(qkv, p["lam"], p["diff_norm"], layer, p["attn_diag"], p["attn_col"], bsz=bsz,
                            n_pad=n_pad)
    o_ssm = ssd_mixer(zx, dtr, p["conv_w"], p["ssd"], layer, p["expand"], bsz=bsz, n_pad=n_pad)
    merged = mm([o_attn, o_ssm], [p["w_proj_attn"], p["w_proj_ssm"]], [0, 1], [gates, gates], _epi_gated_sum,
                ncols=d, tn=512, out_dtype=BF16, extra_col0=(0, d))
    return mm([merged], [p["w_out"]], [0], [h], _epi_residual, ncols=d, tn=512, out_dtype=F32, extra_col0=(0,))


def dense_ffn(h, norm_stack, layer, w_gate, w_up, w_down, group, *, tm):
    m, d = h.shape
    d_ff = w_gate.shape[2]
    grp, nused = _dense_plan(m, tm, group)
    mm = functools.partial(grouped_matmul, grp=grp, nused=nused, tm=tm)
    hn = rmsnorm(h, norm_stack, layer)
    u = mm([hn], [w_gate, w_up], [0, 0], [], _epi_swiglu, ncols=d_ff, tn=256, out_dtype=BF16)
    tm_down = tm // 2
    grp, nused = _dense_plan(m, tm_down, group)
    return grouped_matmul([u], [w_down.astype(BF16)], [0], [h], _epi_residual, grp=grp, nused=nused, ncols=d,
                          tm=tm_down, tn=512, out_dtype=F32, extra_col0=(0,))


def moe_ffn_final(h, norm_stack, layer, router_w, w_gate, w_up, w_down, final_norm, group, *, bsz, n_pad, n_frames):
    d = h.shape[1]
    d_fe = w_gate.shape[3]
    r_pad = jnp.pad(router_w.astype(F32), ((0, 0), (0, LANES - N_EXPERTS)))
    route = route_tokens(h, norm_stack, layer, r_pad)
    tok, tile_grp, rows_used, pos1, pos2 = _moe_plan(route, bsz=bsz, n_pad=n_pad, n_frames=n_frames,
                                                      tm=MOE_MM_ROW_TILE)
    xs = moe_dispatch(h, norm_stack, layer, tok, rows_used // MOE_ROW_TILE)
    mm = functools.partial(grouped_matmul, grp=tile_grp + group * N_EXPERTS, nused=rows_used // MOE_MM_ROW_TILE,
                           tm=MOE_MM_ROW_TILE)
    u = mm([xs], [w_gate.reshape(-1, d, d_fe), w_up.reshape(-1, d, d_fe)], [0, 0], [], _epi_swiglu,
           ncols=d_fe, tn=512, out_dtype=BF16)
    ys = mm([u], [w_down.reshape(-1, d_fe, d)], [0], [], _epi_plain, ncols=d, tn=1024, out_dtype=F32)
    return moe_combine(ys, h, route, final_norm, pos1, pos2, bsz=bsz, n_pad=n_pad, n_frames=n_frames)


def kernel(x, meta_tokens, mix_norm, mix_w_in, conv_w, conv_b, dt_bias, a_log, d_skip, ssm_norm, lambda_q1, lambda_k1, lambda_q2, lambda_k2, diff_norm, w_proj_attn, w_proj_ssm, w_out, ffn_norm, ffn_w_gate, ffn_w_up, ffn_w_down, router, moe_w_gate, moe_w_up, moe_w_down, final_norm):
    bsz, n_frames, d = x.shape
    assert mix_norm.shape[0] == 2 and d == D_MODEL, "layer 0: dense FFN; layer 1: MoE FFN, then the final norm"
    n = N_META + n_frames
    n_pad = -(-n // Q_BLOCK) * Q_BLOCK
    m = bsz * n_pad
    tm = n_pad // 2
    assert tm % 16 == 0 and n_pad % SSD_BLOCK == 0 and n_frames % MOE_ROW_TILE == 0

    p = dict(mix_w_in_t=jnp.swapaxes(mix_w_in, 1, 2), conv_w=conv_w, w_proj_attn=w_proj_attn, w_proj_ssm=w_proj_ssm,
             w_out=w_out)
    p.update(mixer_param_stacks(mix_norm, conv_b, dt_bias, a_log, d_skip, ssm_norm, lambda_q1, lambda_k1, lambda_q2,
                                lambda_k2, diff_norm))
    p.update(mixer_tables(n_pad))
    ffn_norm_stack = ffn_norm.reshape(ffn_norm.shape[0], 1, d)
    h, hn = embed_norm(x, meta_tokens, p["mix_norm"], 0)
    assert h.shape == (m, d)
    h = mixer_layer(h, 0, p, bsz=bsz, n_pad=n_pad, tm=tm, hn=hn)
    h = dense_ffn(h, ffn_norm_stack, 0, ffn_w_gate, ffn_w_up, ffn_w_down, 0, tm=tm)
    h = mixer_layer(h, 1, p, bsz=bsz, n_pad=n_pad, tm=tm)
    out = moe_ffn_final(h, ffn_norm_stack, 1, router[0], moe_w_gate, moe_w_up, moe_w_down, final_norm, 0,
                        bsz=bsz, n_pad=n_pad, n_frames=n_frames)
    return out.reshape(bsz, n_frames, d)
```
